```python
import jax, jax.numpy as jnp
from jax import lax
import numpy as np

D_MODEL = 2048
BATCH = 8
SEQ = 2048
DEPTH = 1

MEM_LEN = 256
EPS = 1e-6

GLA_HEADS = 4
GLA_DK = D_MODEL // (4 * GLA_HEADS)
GLA_DV = D_MODEL // (2 * GLA_HEADS)
GLA_GATE_RANK = 16
GLA_TAU = 16.0
GLA_CHUNK = 64

SG_GROUPS = 4
SG_DIM = D_MODEL // (2 * SG_GROUPS)
SG_CHUNK = 128

XA_HEADS = 4
XA_DIM = D_MODEL // (2 * XA_HEADS)

BRANCH_W = D_MODEL // 2
N_BRANCH = 3
D_FF = 256 * ((8 * D_MODEL // 3 + 255) // 256)

IN_SPLITS = (GLA_HEADS * GLA_DK,
             GLA_HEADS * GLA_DK,
             GLA_HEADS * GLA_DV,
             GLA_HEADS * GLA_DV,
             GLA_GATE_RANK,
             SG_GROUPS * SG_DIM,
             SG_GROUPS * SG_DIM,
             XA_HEADS * XA_DIM,
             N_BRANCH * D_MODEL)
IN_WIDTH = 12304

kernel_name = "hybrid_gla_gmlp_memxattn_macaron"


def rmsnorm(x, g):
    xf = x.astype(jnp.float32)
    y = xf * lax.rsqrt(jnp.mean(xf * xf, axis=-1, keepdims=True) + EPS)
    return (y * g.astype(jnp.float32)).astype(x.dtype)


def swiglu(h, w_gate, w_up, w_down):
    return (jax.nn.silu(h @ w_gate) * (h @ w_up)) @ w_down


def gla(q, k, v, log_a, r, out_norm):
    B, S, H, dk = q.shape
    dv = v.shape[-1]
    C = GLA_CHUNK
    n = S // C

    def to_chunks(t):
        return t.astype(jnp.float32).reshape(B, n, C, H, t.shape[-1]).transpose(1, 0, 3, 2, 4)

    qc = to_chunks(q) * (dk ** -0.5)
    kc, vc, gc = to_chunks(k), to_chunks(v), to_chunks(log_a)
    mask = jnp.tril(jnp.ones((C, C), dtype=bool))[:, :, None]

    def step(state, inp):
        qb, kb, vb, gb = inp
        b = jnp.cumsum(gb, axis=2)
        inter = jnp.einsum('bhtd,bhde->bhte', qb * jnp.exp(b), state)
        diff = b[:, :, :, None, :] - b[:, :, None, :, :]
        decay = jnp.exp(jnp.where(mask, diff, -jnp.inf))
        scores = jnp.einsum('bhtd,bhsd,bhtsd->bhts', qb, kb, decay)
        intra = jnp.einsum('bhts,bhse->bhte', scores, vb)
        b_last = b[:, :, -1:, :]
        new_state = (jnp.exp(b_last[:, :, 0, :])[..., None] * state
                     + jnp.einsum('bhsd,bhse->bhde', kb * jnp.exp(b_last - b), vb))
        return new_state, inter + intra

    s0 = jnp.zeros((B, H, dk, dv), jnp.float32)
    _, o = lax.scan(step, s0, (qc, kc, vc, gc))
    o = o.transpose(1, 0, 3, 2, 4).reshape(B, S, H, dv)
    o = o * lax.rsqrt(jnp.mean(o * o, axis=-1, keepdims=True) + EPS) * out_norm.astype(jnp.float32)
    o = o * jax.nn.silu(r.astype(jnp.float32))
    return o.reshape(B, S, H * dv).astype(v.dtype)


def spatial_gating(u_pre, v_pre, ln_g, ln_b, w_s, b_s):
    B, S, _ = u_pre.shape
    G, c, T = SG_GROUPS, SG_DIM, SG_CHUNK
    u = jax.nn.gelu(u_pre, approximate=False).reshape(B, S, G, c)
    v = jax.nn.gelu(v_pre, approximate=False).reshape(B, S, G, c).astype(jnp.float32)
    mu = jnp.mean(v, axis=-1, keepdims=True)
    var = jnp.mean((v - mu) ** 2, axis=-1, keepdims=True)
    v = (v - mu) * lax.rsqrt(var + EPS) * ln_g.astype(jnp.float32) + ln_b.astype(jnp.float32)
    v = v.reshape(B, S // T, T, G, c)
    w = w_s.astype(jnp.float32) * jnp.tril(jnp.ones((T, T), jnp.float32))
    v = jnp.einsum('gts,bnsgc->bntgc', w, v) + b_s.astype(jnp.float32).T[:, :, None]
    out = u.astype(jnp.float32) * v.reshape(B, S, G, c)
    return out.reshape(B, S, G * c).astype(u_pre.dtype)


def memory_attention(q, mem_n, w_kv_mem):
    B, S, H, dh = q.shape
    M = mem_n.shape[1]
    kv = mem_n @ w_kv_mem
    k, v = jnp.split(kv, 2, axis=-1)
    k = k.reshape(B, M, H, dh)
    v = v.reshape(B, M, H, dh)
    s = jnp.einsum('bshd,bmhd->bhsm', q, k).astype(jnp.float32) * (dh ** -0.5)
    p = jax.nn.softmax(s, axis=-1).astype(v.dtype)
    o = jnp.einsum('bhsm,bmhd->bshd', p, v)
    return o.reshape(B, S, H * dh)


def mixer(h, mem_n, w_in, gla_w_gate_up, gla_gate_bias, gla_out_norm,
          sg_ln_g, sg_ln_b, sg_w_s, sg_b_s, w_kv_mem, w_branch, w_out):
    B, S, D = h.shape
    offsets = []
    acc = 0
    for width in IN_SPLITS[:-1]:
        acc += width
        offsets.append(acc)
    proj = h @ w_in
    q, k, v, r, g_lr, su, sv, xq, gate_pre = jnp.split(proj, offsets, axis=-1)

    log_a = jax.nn.log_sigmoid((g_lr @ gla_w_gate_up + gla_gate_bias).astype(jnp.float32)) / GLA_TAU
    y_gla = gla(q.reshape(B, S, GLA_HEADS, GLA_DK), k.reshape(B, S, GLA_HEADS, GLA_DK),
                v.reshape(B, S, GLA_HEADS, GLA_DV), log_a.reshape(B, S, GLA_HEADS, GLA_DK),
                r.reshape(B, S, GLA_HEADS, GLA_DV), gla_out_norm)
    y_sg = spatial_gating(su, sv, sg_ln_g, sg_ln_b, sg_w_s, sg_b_s)
    y_xa = memory_attention(xq.reshape(B, S, XA_HEADS, XA_DIM), mem_n, w_kv_mem)

    gates = jax.nn.sigmoid(gate_pre.astype(jnp.float32)).reshape(B, S, N_BRANCH, D)
    merged = jnp.zeros((B, S, D), jnp.float32)
    for i, y in enumerate((y_gla, y_sg, y_xa)):
        merged = merged + gates[:, :, i, :] * (y @ w_branch[i]).astype(jnp.float32)
    return merged.astype(h.dtype) @ w_out


def setup_inputs(seed: int = 0) -> dict:
    key = jax.random.key(seed)
    ks = jax.random.split(key, 24)
    f32 = jnp.float32
    L, D = DEPTH, D_MODEL

    def nrm(k, shape, scale):
        return jax.random.normal(k, shape, f32) * scale

    def gain(k, shape):
        return 1.0 + 0.01 * jax.random.normal(k, shape, f32)

    return {
        "x": jax.random.normal(ks[0], (BATCH, SEQ, D), f32),
        "mem": jax.random.normal(ks[1], (BATCH, MEM_LEN, D), f32),
        "ffn1_norm": gain(ks[2], (L, D)),
        "ffn1_w_gate": nrm(ks[3], (L, D, D_FF), D ** -0.5),
        "ffn1_w_up": nrm(ks[4], (L, D, D_FF), D ** -0.5),
        "ffn1_w_down": nrm(ks[5], (L, D_FF, D), D_FF ** -0.5),
        "mix_norm": gain(ks[6], (L, D)),
        "mem_norm": gain(ks[7], (L, D)),
        "w_in": nrm(ks[8], (L, D, IN_WIDTH), D ** -0.5),
        "gla_w_gate_up": nrm(ks[9], (L, GLA_GATE_RANK, GLA_HEADS * GLA_DK), GLA_GATE_RANK ** -0.5),
        "gla_gate_bias": nrm(ks[10], (L, GLA_HEADS * GLA_DK), 0.01),
        "gla_out_norm": gain(ks[11], (L, GLA_HEADS, GLA_DV)),
        "sg_ln_g": gain(ks[12], (L, SG_GROUPS, SG_DIM)),
        "sg_ln_b": nrm(ks[13], (L, SG_GROUPS, SG_DIM), 0.01),
        "sg_w_s": nrm(ks[14], (L, SG_GROUPS, SG_CHUNK, SG_CHUNK), SG_CHUNK ** -0.5),
        "sg_b_s": gain(ks[15], (L, SG_GROUPS, SG_CHUNK)),
        "w_kv_mem": nrm(ks[16], (L, D, 2 * XA_HEADS * XA_DIM), D ** -0.5),
        "w_branch": nrm(ks[17], (L, N_BRANCH, BRANCH_W, D), BRANCH_W ** -0.5),
        "w_out": nrm(ks[18], (L, D, D), D ** -0.5),
        "ffn2_norm": gain(ks[19], (L, D)),
        "ffn2_w_gate": nrm(ks[20], (L, D, D_FF), D ** -0.5),
        "ffn2_w_up": nrm(ks[21], (L, D, D_FF), D ** -0.5),
        "ffn2_w_down": nrm(ks[22], (L, D_FF, D), D_FF ** -0.5),
        "final_norm": gain(ks[23], (D,)),
    }


def reference(x, mem, ffn1_norm, ffn1_w_gate, ffn1_w_up, ffn1_w_down, mix_norm, mem_norm,
              w_in, gla_w_gate_up, gla_gate_bias, gla_out_norm, sg_ln_g, sg_ln_b, sg_w_s, sg_b_s,
              w_kv_mem, w_branch, w_out, ffn2_norm, ffn2_w_gate, ffn2_w_up, ffn2_w_down, final_norm):
    for l in range(DEPTH):
        x = x + 0.5 * swiglu(rmsnorm(x, ffn1_norm[l]), ffn1_w_gate[l], ffn1_w_up[l], ffn1_w_down[l])
        h = rmsnorm(x, mix_norm[l])
        mem_n = rmsnorm(mem, mem_norm[l])
        x = x + mixer(h, mem_n, w_in[l], gla_w_gate_up[l], gla_gate_bias[l], gla_out_norm[l],
                      sg_ln_g[l], sg_ln_b[l], sg_w_s[l], sg_b_s[l], w_kv_mem[l], w_branch[l], w_out[l])
        x = x + 0.5 * swiglu(rmsnorm(x, ffn2_norm[l]), ffn2_w_gate[l], ffn2_w_up[l], ffn2_w_down[l])
    return rmsnorm(x, final_norm)
```

```python
import functools

import jax
import jax.numpy as jnp
from jax import lax
from jax.experimental import pallas as pl
from jax.experimental.pallas import tpu as pltpu

F32 = jnp.float32
BF16 = jnp.bfloat16
HIGHEST = lax.Precision.HIGHEST
NT_DIMS = (((1,), (1,)), ((), ()))
TN_DIMS = (((0,), (0,)), ((), ()))

EPS = 1e-6
GLA_HEADS = 4
GLA_GATE_RANK = 16
GLA_TAU = 16.0
GLA_CHUNK = 64
GLA_SUB = 16
SG_GROUPS = 4
SG_CHUNK = 128
XA_HEADS = 4
N_BRANCH = 3

LANES = 128
VMEM_LIMIT = 56 * 1024 * 1024


def _params(semantics):
    return pltpu.CompilerParams(dimension_semantics=semantics,
                                vmem_limit_bytes=VMEM_LIMIT)


def _rms(x, g):
    return x * lax.rsqrt(jnp.mean(x * x, axis=-1, keepdims=True) + EPS) * g


def _ffn_body(x_ref, g_ref, wg_ref, wu_ref, wd_ref, pg_ref, *refs, emit_resid):
    if emit_resid:
        xo_ref, no_ref, xn_ref, acc_ref = refs
    else:
        no_ref, xn_ref, acc_ref = refs
    f = pl.program_id(1)

    @pl.when(f == 0)
    def _():
        xn_ref[...] = _rms(x_ref[...], g_ref[...]).astype(BF16)

    xn = xn_ref[...]
    gate = jnp.dot(xn, wg_ref[...], preferred_element_type=F32)
    up = jnp.dot(xn, wu_ref[...], preferred_element_type=F32)
    hid = (jax.nn.silu(gate) * up).astype(BF16)
    part = jnp.dot(hid, wd_ref[...], preferred_element_type=F32)

    @pl.when(f == 0)
    def _():
        acc_ref[...] = part

    @pl.when(f > 0)
    def _():
        acc_ref[...] += part

    @pl.when(f == pl.num_programs(1) - 1)
    def _():
        y = x_ref[...] + 0.5 * acc_ref[...]
        if emit_resid:
            xo_ref[...] = y
        no_ref[...] = _rms(y, pg_ref[...]).astype(no_ref.dtype)


def _ffn(x, g, wg, wu, wd, post_g, *, emit_resid, tm=512, tf=512):
    n, d = x.shape
    dff = wg.shape[1]
    row = lambda i, f: (i, 0)
    out_shape = [jax.ShapeDtypeStruct((n, d), BF16 if emit_resid else F32)]
    out_specs = [pl.BlockSpec((tm, d), row)]
    if emit_resid:
        out_shape.insert(0, jax.ShapeDtypeStruct((n, d), F32))
        out_specs.insert(0, pl.BlockSpec((tm, d), row))
    return pl.pallas_call(
        functools.partial(_ffn_body, emit_resid=emit_resid),
        grid=(n // tm, dff // tf),
        in_specs=[
            pl.BlockSpec((tm, d), row),
            pl.BlockSpec((1, d), lambda i, f: (0, 0)),
            pl.BlockSpec((d, tf), lambda i, f: (0, f)),
            pl.BlockSpec((d, tf), lambda i, f: (0, f)),
            pl.BlockSpec((tf, d), lambda i, f: (f, 0)),
            pl.BlockSpec((1, d), lambda i, f: (0, 0)),
        ],
        out_specs=out_specs,
        out_shape=out_shape,
        scratch_shapes=[pltpu.VMEM((tm, d), BF16), pltpu.VMEM((tm, d), F32)],
        compiler_params=_params(("parallel", "arbitrary")),
        name="ffn",
    )(x, g, wg, wu, wd, post_g)


def _matmul_body(x_ref, w_ref, o_ref):
    o_ref[...] = jnp.dot(x_ref[...], w_ref[...],
                         preferred_element_type=F32).astype(o_ref.dtype)


def _matmul(x, w, out_dtype, *, tm, tn, name):
    n, d = x.shape
    nc = w.shape[1]
    return pl.pallas_call(
        _matmul_body,
        grid=(n // tm, nc // tn),
        in_specs=[pl.BlockSpec((tm, d), lambda i, j: (i, 0)),
                  pl.BlockSpec((d, tn), lambda i, j: (0, j))],
        out_specs=pl.BlockSpec((tm, tn), lambda i, j: (i, j)),
        out_shape=jax.ShapeDtypeStruct((n, nc), out_dtype),
        compiler_params=_params(("parallel", "parallel")),
        name=name,
    )(x, w)


def _norm_matmul_body(x_ref, g_ref, w_ref, o_ref):
    xn = _rms(x_ref[...], g_ref[...]).astype(BF16)
    o_ref[...] = jnp.dot(xn, w_ref[...], preferred_element_type=F32).astype(o_ref.dtype)


def _norm_matmul(x, g, w, out_dtype, *, tm, name):
    n, d = x.shape
    nc = w.shape[1]
    return pl.pallas_call(
        _norm_matmul_body,
        grid=(n // tm,),
        in_specs=[pl.BlockSpec((tm, d), lambda i: (i, 0)),
                  pl.BlockSpec((1, d), lambda i: (0, 0)),
                  pl.BlockSpec((d, nc), lambda i: (0, 0))],
        out_specs=pl.BlockSpec((tm, nc), lambda i: (i, 0)),
        out_shape=jax.ShapeDtypeStruct((n, nc), out_dtype),
        compiler_params=_params(("parallel",)),
        name=name,
    )(x, g, w)


def _gla_body(q_ref, k_ref, v_ref, r_ref, glr_ref, wgu_ref, gb_ref, on_ref,
              y_ref, st_ref, la_ref, *, heads, chunk, sub):
    ts = q_ref.shape[0]
    dk = q_ref.shape[1] // heads
    dv = v_ref.shape[1] // heads
    scale = dk ** -0.5

    @pl.when(pl.program_id(1) == 0)
    def _():
        st_ref[...] = jnp.zeros_like(st_ref)

    z = jnp.dot(glr_ref[...], wgu_ref[...], preferred_element_type=F32,
                precision=HIGHEST) + gb_ref[...]
    la_ref[...] = jax.nn.log_sigmoid(z) * (1.0 / GLA_TAU)

    cc_r = lax.broadcasted_iota(jnp.int32, (chunk, chunk), 0)
    cc_c = lax.broadcasted_iota(jnp.int32, (chunk, chunk), 1)
    tril = (cc_r >= cc_c).astype(F32)
    krow = lax.broadcasted_iota(jnp.int32, (chunk, dk), 0)
    srow = lax.broadcasted_iota(jnp.int32, (sub, dk), 0)
    slane = lax.broadcasted_iota(jnp.int32, (sub, chunk), 1)

    def chunk_step(c, carry):
        r0 = pl.multiple_of(c * chunk, chunk)
        rows = pl.ds(r0, chunk)
        b_all = jnp.dot(tril, la_ref[rows, :], preferred_element_type=F32,
                        precision=HIGHEST)
        for h in range(heads):
            kcols = slice(h * dk, (h + 1) * dk)
            vcols = slice(h * dv, (h + 1) * dv)
            b = b_all[:, kcols]
            q = q_ref[rows, kcols].astype(F32) * scale
            k = k_ref[rows, kcols].astype(F32)
            v = v_ref[rows, vcols]
            st = st_ref[h]
            inter = lax.dot_general((q * jnp.exp(b)).astype(BF16), st.astype(BF16),
                                    NT_DIMS, preferred_element_type=F32)
            blocks = []
            for i in range(chunk // sub):
                lo = i * sub
                qi, ki, bi = q[lo:lo + sub], k[lo:lo + sub], b[lo:lo + sub]
                sc = jnp.zeros((sub, chunk), F32)
                for s in range(sub):
                    dec = jnp.exp(jnp.where(srow >= s, bi - bi[s:s + 1], -jnp.inf))
                    col = jnp.sum(qi * ki[s:s + 1] * dec, axis=-1, keepdims=True)
                    sc = sc + jnp.where(slane == lo + s, col, 0.0)
                if i > 0:
                    edge = b[lo - 1:lo]
                    qd = qi * jnp.exp(bi - edge)
                    kd = k * jnp.exp(jnp.where(krow < lo, edge - b, -jnp.inf))
                    sc = sc + lax.dot_general(qd.astype(BF16), kd.astype(BF16),
                                              NT_DIMS, preferred_element_type=F32)
                blocks.append(sc)
            scores = jnp.concatenate(blocks, axis=0)
            o = inter + jnp.dot(scores.astype(BF16), v, preferred_element_type=F32)

            b_end = b[chunk - 1:chunk]
            k_end = (k * jnp.exp(b_end - b)).astype(BF16)
            upd = lax.dot_general(v, k_end, TN_DIMS, preferred_element_type=F32)
            st_ref[h] = st * jnp.exp(b_end) + upd

            o = o * lax.rsqrt(jnp.mean(o * o, axis=-1, keepdims=True) + EPS)
            o = o * on_ref[:, vcols] * jax.nn.silu(r_ref[rows, vcols].astype(F32))
            y_ref[rows, vcols] = o.astype(y_ref.dtype)
        return carry

    lax.fori_loop(0, ts // chunk, chunk_step, 0)


def _gla(proj, glr, wgu, gbias, onorm, *, batch, seq, ts=256):
    hk = wgu.shape[1]
    hv = onorm.shape[1]
    ns = seq // ts
    row = lambda b, s: b * ns + s
    return pl.pallas_call(
        functools.partial(_gla_body, heads=GLA_HEADS, chunk=GLA_CHUNK, sub=GLA_SUB),
        grid=(batch, ns),
        in_specs=[
            pl.BlockSpec((ts, hk), lambda b, s: (row(b, s), 0)),
            pl.BlockSpec((ts, hk), lambda b, s: (row(b, s), 1)),
            pl.BlockSpec((ts, hv), lambda b, s: (row(b, s), hk * 2 // hv)),
            pl.BlockSpec((ts, hv), lambda b, s: (row(b, s), hk * 2 // hv + 1)),
            pl.BlockSpec((ts, LANES), lambda b, s: (row(b, s), 0)),
            pl.BlockSpec((LANES, hk), lambda b, s: (0, 0)),
            pl.BlockSpec((1, hk), lambda b, s: (0, 0)),
            pl.BlockSpec((1, hv), lambda b, s: (0, 0)),
        ],
        out_specs=pl.BlockSpec((ts, hv), lambda b, s: (row(b, s), 0)),
        out_shape=jax.ShapeDtypeStruct((batch * seq, hv), BF16),
        scratch_shapes=[pltpu.VMEM((GLA_HEADS, hv // GLA_HEADS, hk // GLA_HEADS), F32),
                        pltpu.VMEM((ts, hk), F32)],
        compiler_params=_params(("parallel", "arbitrary")),
        name="gla",
    )(proj, proj, proj, proj, glr, wgu, gbias, onorm)


def _gelu(x):
    return 0.5 * x * (1.0 + lax.erf(x * (2.0 ** -0.5)))


def _sg_body(u_ref, v_ref, lg_ref, lb_ref, ws_ref, bs_ref, y_ref, *, groups, chunk):
    ts = u_ref.shape[0]
    gdim = u_ref.shape[1] // groups
    rr = lax.broadcasted_iota(jnp.int32, (chunk, chunk), 0)
    cc = lax.broadcasted_iota(jnp.int32, (chunk, chunk), 1)
    for g in range(groups):
        cols = slice(g * gdim, (g + 1) * gdim)
        w = jnp.where(rr >= cc, ws_ref[g], 0.0).astype(BF16)
        for c in range(ts // chunk):
            rows = slice(c * chunk, (c + 1) * chunk)
            u = _gelu(u_ref[rows, cols].astype(F32))
            v = _gelu(v_ref[rows, cols].astype(F32))
            mu = jnp.mean(v, axis=-1, keepdims=True)
            var = jnp.mean((v - mu) ** 2, axis=-1, keepdims=True)
            vn = (v - mu) * lax.rsqrt(var + EPS) * lg_ref[g:g + 1, :] + lb_ref[g:g + 1, :]
            vs = jnp.dot(w, vn.astype(BF16), preferred_element_type=F32) + bs_ref[g]
            y_ref[rows, cols] = (u * vs).astype(y_ref.dtype)


def _sgate(proj, ln_g, ln_b, w_s, b_s, *, u_block, ts=256):
    n = proj.shape[0]
    groups, gdim = ln_g.shape
    width = groups * gdim
    chunk = w_s.shape[1]
    return pl.pallas_call(
        functools.partial(_sg_body, groups=groups, chunk=chunk),
        grid=(n // ts,),
        in_specs=[
            pl.BlockSpec((ts, width), lambda i: (i, u_block)),
            pl.BlockSpec((ts, width), lambda i: (i, u_block + 1)),
            pl.BlockSpec((groups, gdim), lambda i: (0, 0)),
            pl.BlockSpec((groups, gdim), lambda i: (0, 0)),
            pl.BlockSpec((groups, chunk, chunk), lambda i: (0, 0, 0)),
            pl.BlockSpec((groups, chunk, 1), lambda i: (0, 0, 0)),
        ],
        out_specs=pl.BlockSpec((ts, width), lambda i: (i, 0)),
        out_shape=jax.ShapeDtypeStruct((n, width), BF16),
        compiler_params=_params(("parallel",)),
        name="sgate",
    )(proj, proj, ln_g, ln_b, w_s, b_s)


def _xa_body(q_ref, k_ref, v_ref, y_ref, *, heads):
    dh = q_ref.shape[1] // heads
    for h in range(heads):
        cols = slice(h * dh, (h + 1) * dh)
        s = lax.dot_general(q_ref[:, cols], k_ref[:, cols], NT_DIMS,
                            preferred_element_type=F32) * (dh ** -0.5)
        p = jnp.exp(s - jnp.max(s, axis=-1, keepdims=True))
        p = p / jnp.sum(p, axis=-1, keepdims=True)
        y_ref[:, cols] = jnp.dot(p.astype(BF16), v_ref[:, cols],
                                 preferred_element_type=F32).astype(y_ref.dtype)


def _xattn(proj, kv, *, q_block, batch, seq, ts=512):
    mem_len = kv.shape[0] // batch
    width = kv.shape[1] // 2
    ns = seq // ts
    return pl.pallas_call(
        functools.partial(_xa_body, heads=XA_HEADS),
        grid=(batch, ns),
        in_specs=[
            pl.BlockSpec((ts, width), lambda b, s: (b * ns + s, q_block)),
            pl.BlockSpec((mem_len, width), lambda b, s: (b, 0)),
            pl.BlockSpec((mem_len, width), lambda b, s: (b, 1)),
        ],
        out_specs=pl.BlockSpec((ts, width), lambda b, s: (b * ns + s, 0)),
        out_shape=jax.ShapeDtypeStruct((batch * seq, width), BF16),
        compiler_params=_params(("parallel", "parallel")),
        name="xattn",
    )(proj, kv, kv)


def _merge_body(ya_ref, yb_ref, yc_ref, ga_ref, gb_ref, gc_ref, wb_ref, wo_ref,
                x_ref, o_ref):
    merged = None
    for i, (y_ref, g_ref) in enumerate(((ya_ref, ga_ref), (yb_ref, gb_ref), (yc_ref, gc_ref))):
        t = jnp.dot(y_ref[...], wb_ref[i], preferred_element_type=F32)
        t = jax.nn.sigmoid(g_ref[...].astype(F32)) * t
        merged = t if merged is None else merged + t
    o_ref[...] = x_ref[...] + jnp.dot(merged.astype(BF16), wo_ref[...],
                                      preferred_element_type=F32)


def _merge(ya, yb, yc, proj, wb, wo, x, *, gate_block, tm=256):
    n, d = x.shape
    bw = ya.shape[1]
    row = lambda i: (i, 0)
    once = pl.Buffered(1)
    return pl.pallas_call(
        _merge_body,
        grid=(n // tm,),
        in_specs=[
            pl.BlockSpec((tm, bw), row), pl.BlockSpec((tm, bw), row), pl.BlockSpec((tm, bw), row),
            pl.BlockSpec((tm, d), lambda i: (i, gate_block)),
            pl.BlockSpec((tm, d), lambda i: (i, gate_block + 1)),
            pl.BlockSpec((tm, d), lambda i: (i, gate_block + 2)),
            pl.BlockSpec(wb.shape, lambda i: (0, 0, 0), pipeline_mode=once),
            pl.BlockSpec(wo.shape, lambda i: (0, 0), pipeline_mode=once),
            pl.BlockSpec((tm, d), row),
        ],
        out_specs=pl.BlockSpec((tm, d), row),
        out_shape=jax.ShapeDtypeStruct((n, d), F32),
        compiler_params=_params(("parallel",)),
        name="merge",
    )(ya, yb, yc, proj, proj, proj, wb, wo, x)


def kernel(x, mem, ffn1_norm, ffn1_w_gate, ffn1_w_up, ffn1_w_down, mix_norm, mem_norm, w_in, gla_w_gate_up, gla_gate_bias, gla_out_norm, sg_ln_g, sg_ln_b, sg_w_s, sg_b_s, w_kv_mem, w_branch, w_out, ffn2_norm, ffn2_w_gate, ffn2_w_up, ffn2_w_down, final_norm):
    batch, seq, d = x.shape
    depth = w_in.shape[0]
    hk = gla_w_gate_up.shape[2]
    hv = gla_out_norm.shape[1] * gla_out_norm.shape[2]
    bw = d // 2
    rank = gla_w_gate_up.shape[1]
    lr_lo = 2 * hk + 2 * hv

    xf = x.reshape(batch * seq, d)
    memf = mem.reshape(-1, d)
    for l in range(depth):
        last = l == depth - 1
        x1, h = _ffn(xf, ffn1_norm[l][None], ffn1_w_gate[l].astype(BF16),
                     ffn1_w_up[l].astype(BF16), ffn1_w_down[l].astype(BF16),
                     mix_norm[l][None], emit_resid=True)

        w_main = jnp.concatenate([w_in[l][:, :lr_lo], w_in[l][:, lr_lo + rank:]],
                                 axis=1).astype(BF16)
        w_lr = jnp.pad(w_in[l][:, lr_lo:lr_lo + rank], ((0, 0), (0, LANES - rank))).astype(BF16)
        proj = _matmul(h, w_main, BF16, tm=1024, tn=1024, name="proj")
        glr = _matmul(h, w_lr, F32, tm=1024, tn=LANES, name="proj_lr")
        kv = _norm_matmul(memf, mem_norm[l][None], w_kv_mem[l].astype(BF16), BF16,
                          tm=512, name="mem_kv")

        wgu = jnp.pad(gla_w_gate_up[l], ((0, LANES - rank), (0, 0)))
        y_gla = _gla(proj, glr, wgu, gla_gate_bias[l][None], gla_out_norm[l].reshape(1, hv),
                     batch=batch, seq=seq)
        y_sg = _sgate(proj, sg_ln_g[l], sg_ln_b[l], sg_w_s[l], sg_b_s[l][:, :, None],
                      u_block=lr_lo // bw)
        y_xa = _xattn(proj, kv, q_block=lr_lo // bw + 2, batch=batch, seq=seq)
        x2 = _merge(y_gla, y_sg, y_xa, proj, w_branch[l].astype(BF16), w_out[l].astype(BF16),
                    x1, gate_block=(lr_lo + 3 * bw) // d)

        post = final_norm[None] if last else ffn1_norm[l + 1][None]
        res = _ffn(x2, ffn2_norm[l][None], ffn2_w_gate[l].astype(BF16),
                   ffn2_w_up[l].astype(BF16), ffn2_w_down[l].astype(BF16),
                   post, emit_resid=not last)
        xf = res[0]
    return xf.reshape(batch, seq, d)
```

```python
import functools

import jax
import jax.numpy as jnp
from jax import lax
from jax.experimental import pallas as pl
from jax.experimental.pallas import tpu as pltpu

F32 = jnp.float32
BF16 = jnp.bfloat16
HIGHEST = lax.Precision.HIGHEST
NT_DIMS = (((1,), (1,)), ((), ()))
TN_DIMS = (((0,), (0,)), ((), ()))

EPS = 1e-6
GLA_HEADS = 4
GLA_TAU = 16.0
GLA_CHUNK = 128
GLA_SUB = 8
XA_HEADS = 4

LANES = 128
VMEM_LIMIT = 56 * 1024 * 1024


def _params(semantics):
    return pltpu.CompilerParams(dimension_semantics=semantics,
                                vmem_limit_bytes=VMEM_LIMIT)


def _rms(x, g):
    return x * lax.rsqrt(jnp.mean(x * x, axis=-1, keepdims=True) + EPS) * g


def _ffn_body(x_ref, g_ref, wg_ref, wu_ref, wd_ref, pg_ref, *refs, emit_resid):
    if emit_resid:
        xo_ref, no_ref, xn_ref, acc_ref = refs
    else:
        no_ref, xn_ref, acc_ref = refs
    f = pl.program_id(1)

    @pl.when(f == 0)
    def _():
        xn_ref[...] = _rms(x_ref[...], g_ref[...]).astype(BF16)
        acc_ref[...] = jnp.zeros_like(acc_ref)

    xn = xn_ref[...]
    gate = jnp.dot(xn, wg_ref[...], preferred_element_type=F32)
    up = jnp.dot(xn, wu_ref[...], preferred_element_type=F32)
    hid = (jax.nn.silu(gate) * up).astype(BF16)
    acc_ref[...] += jnp.dot(hid, wd_ref[...], preferred_element_type=F32)

    @pl.when(f == pl.num_programs(1) - 1)
    def _():
        y = x_ref[...] + 0.5 * acc_ref[...]
        if emit_resid:
            xo_ref[...] = y
        no_ref[...] = _rms(y, pg_ref[...]).astype(no_ref.dtype)


def _ffn(x, g, wg, wu, wd, post_g, *, emit_resid, tm=512, tf=512):
    n, d = x.shape
    dff = wg.shape[1]
    row = lambda i, f: (i, 0)
    out_shape = [jax.ShapeDtypeStruct((n, d), BF16 if emit_resid else F32)]
    out_specs = [pl.BlockSpec((tm, d), row)]
    if emit_resid:
        out_shape.insert(0, jax.ShapeDtypeStruct((n, d), F32))
        out_specs.insert(0, pl.BlockSpec((tm, d), row))
    return pl.pallas_call(
        functools.partial(_ffn_body, emit_resid=emit_resid),
        grid=(n // tm, dff // tf),
        in_specs=[
            pl.BlockSpec((tm, d), row),
            pl.BlockSpec((1, d), lambda i, f: (0, 0)),
            pl.BlockSpec((d, tf), lambda i, f: (0, f)),
            pl.BlockSpec((d, tf), lambda i, f: (0, f)),
            pl.BlockSpec((tf, d), lambda i, f: (f, 0)),
            pl.BlockSpec((1, d), lambda i, f: (0, 0)),
        ],
        out_specs=out_specs,
        out_shape=out_shape,
        scratch_shapes=[pltpu.VMEM((tm, d), BF16), pltpu.VMEM((tm, d), F32)],
        compiler_params=_params(("parallel", "arbitrary")),
        name="ffn",
    )(x, g, wg, wu, wd, post_g)


def _regroup_body(w_ref, o_ref, *, cut_lo, cut_hi):
    o_ref[:, :cut_lo] = w_ref[:, :cut_lo].astype(BF16)
    o_ref[:, cut_lo:] = w_ref[:, cut_hi:].astype(BF16)


def _regroup(w, *, cut_lo, cut_hi, tr=128):
    d, width = w.shape
    out_w = width - (cut_hi - cut_lo)
    return pl.pallas_call(
        functools.partial(_regroup_body, cut_lo=cut_lo, cut_hi=cut_hi),
        grid=(d // tr,),
        in_specs=[pl.BlockSpec((tr, width), lambda i: (i, 0))],
        out_specs=pl.BlockSpec((tr, out_w), lambda i: (i, 0)),
        out_shape=jax.ShapeDtypeStruct((d, out_w), BF16),
        compiler_params=_params(("parallel",)),
        name="regroup",
    )(w)


def _matmul_body(x_ref, w_ref, o_ref, *, precision):
    o_ref[...] = jnp.dot(x_ref[...], w_ref[...], preferred_element_type=F32,
                         precision=precision).astype(o_ref.dtype)


def _matmul(x, w, out_dtype, *, tm, tn, name, precision=None):
    n, d = x.shape
    nc = w.shape[1]
    return pl.pallas_call(
        functools.partial(_matmul_body, precision=precision),
        grid=(n // tm, nc // tn),
        in_specs=[pl.BlockSpec((tm, d), lambda i, j: (i, 0)),
                  pl.BlockSpec((d, tn), lambda i, j: (0, j))],
        out_specs=pl.BlockSpec((tm, tn), lambda i, j: (i, j)),
        out_shape=jax.ShapeDtypeStruct((n, nc), out_dtype),
        compiler_params=_params(("parallel", "parallel")),
        name=name,
    )(x, w)


def _norm_matmul_body(x_ref, g_ref, w_ref, o_ref):
    xn = _rms(x_ref[...], g_ref[...]).astype(BF16)
    o_ref[...] = jnp.dot(xn, w_ref[...], preferred_element_type=F32).astype(o_ref.dtype)


def _norm_matmul(x, g, w, out_dtype, *, tm, name):
    n, d = x.shape
    nc = w.shape[1]
    return pl.pallas_call(
        _norm_matmul_body,
        grid=(n // tm,),
        in_specs=[pl.BlockSpec((tm, d), lambda i: (i, 0)),
                  pl.BlockSpec((1, d), lambda i: (0, 0)),
                  pl.BlockSpec((d, nc), lambda i: (0, 0))],
        out_specs=pl.BlockSpec((tm, nc), lambda i: (i, 0)),
        out_shape=jax.ShapeDtypeStruct((n, nc), out_dtype),
        compiler_params=_params(("parallel",)),
        name=name,
    )(x, g, w)


def _la_body(x_ref, w_ref, b_ref, o_ref):
    z = jnp.dot(x_ref[...], w_ref[...], preferred_element_type=F32) + b_ref[...]
    o_ref[...] = jax.nn.log_sigmoid(z) * (1.0 / GLA_TAU)


def _la_proj(x, w, bias, *, tm=1024):
    n, d = x.shape
    nc = w.shape[1]
    return pl.pallas_call(
        _la_body,
        grid=(n // tm,),
        in_specs=[pl.BlockSpec((tm, d), lambda i: (i, 0)),
                  pl.BlockSpec((d, nc), lambda i: (0, 0)),
                  pl.BlockSpec((1, nc), lambda i: (0, 0))],
        out_specs=pl.BlockSpec((tm, nc), lambda i: (i, 0)),
        out_shape=jax.ShapeDtypeStruct((n, nc), F32),
        compiler_params=_params(("parallel",)),
        name="la_proj",
    )(x, w, bias)


def _gla_body(q_ref, k_ref, v_ref, r_ref, la_ref, on_ref, y_ref, st_ref,
              *, heads, chunk, sub):
    ts = q_ref.shape[0]
    dk = q_ref.shape[1] // heads
    dv = v_ref.shape[1] // heads
    scale = dk ** -0.5

    @pl.when(pl.program_id(1) == 0)
    def _():
        st_ref[...] = jnp.zeros_like(st_ref)

    t_idx = lax.broadcasted_iota(jnp.int32, (chunk, chunk), 0)
    s_idx = lax.broadcasted_iota(jnp.int32, (chunk, chunk), 1)
    causal = t_idx >= s_idx
    ones_tril = causal.astype(BF16)
    row = lax.broadcasted_iota(jnp.int32, (chunk, dk), 0)
    lane_in_sub = lax.broadcasted_iota(jnp.int32, (sub, chunk), 1) % sub

    halves = []
    m = chunk // 2
    while m >= sub:
        halves.append(m)
        m //= 2

    def chunk_step(c, carry):
        r0 = pl.multiple_of(c * chunk, chunk)
        rows = pl.ds(r0, chunk)
        la = la_ref[rows, :]
        la_hi = la.astype(BF16)
        la_lo = (la - la_hi.astype(F32)).astype(BF16)
        b_all = (jnp.dot(ones_tril, la_hi, preferred_element_type=F32)
                 + jnp.dot(ones_tril, la_lo, preferred_element_type=F32))
        for h in range(heads):
            kcols = slice(h * dk, (h + 1) * dk)
            vcols = slice(h * dv, (h + 1) * dv)
            b = b_all[:, kcols]
            q = q_ref[rows, kcols].astype(F32) * scale
            k = k_ref[rows, kcols].astype(F32)
            v = v_ref[rows, vcols]
            st = st_ref[h]
            inter = lax.dot_general((q * jnp.exp(b)).astype(BF16), st.astype(BF16),
                                    NT_DIMS, preferred_element_type=F32)

            diag = []
            for i in range(chunk // sub):
                lo = i * sub
                qi, ki, bi = q[lo:lo + sub], k[lo:lo + sub], b[lo:lo + sub]
                d = jnp.zeros((sub, chunk), F32)
                for s in range(sub):
                    dec = jnp.exp(bi - bi[s:s + 1])
                    col = jnp.sum(qi * ki[s:s + 1] * dec, axis=-1, keepdims=True)
                    d = jnp.where(lane_in_sub == s, col, d)
                diag.append(d)
            scores = jnp.concatenate(diag, axis=0)

            for m in halves:
                lg = m.bit_length() - 1
                upper = ((row >> lg) & 1) == 1
                b_mid = jnp.concatenate(
                    [jnp.broadcast_to(b[j + m - 1:j + m], (2 * m, dk))
                     for j in range(0, chunk, 2 * m)], axis=0)
                x = jnp.where(upper, b - b_mid, b_mid - b)
                z = (jnp.where(upper, q, k) * jnp.exp(x)).astype(BF16)
                pair = lax.dot_general(z, z, NT_DIMS, preferred_element_type=F32)
                here = (((t_idx >> lg) ^ (s_idx >> lg)) == 1) & (t_idx > s_idx)
                scores = jnp.where(here, pair, scores)
            scores = jnp.where(causal, scores, 0.0)
            o = inter + jnp.dot(scores.astype(BF16), v, preferred_element_type=F32)

            b_end = b[chunk - 1:chunk]
            k_end = (k * jnp.exp(b_end - b)).astype(BF16)
            upd = lax.dot_general(v, k_end, TN_DIMS, preferred_element_type=F32)
            st_ref[h] = st * jnp.exp(b_end) + upd

            o = o * lax.rsqrt(jnp.mean(o * o, axis=-1, keepdims=True) + EPS)
            o = o * on_ref[:, vcols] * jax.nn.silu(r_ref[rows, vcols].astype(F32))
            y_ref[rows, vcols] = o.astype(y_ref.dtype)
        return carry

    lax.fori_loop(0, ts // chunk, chunk_step, 0)


def _gla(proj, la, onorm, *, batch, seq, ts=512):
    hk = la.shape[1]
    hv = onorm.shape[1]
    ns = seq // ts
    row = lambda b, s: b * ns + s
    return pl.pallas_call(
        functools.partial(_gla_body, heads=GLA_HEADS, chunk=GLA_CHUNK, sub=GLA_SUB),
        grid=(batch, ns),
        in_specs=[
            pl.BlockSpec((ts, hk), lambda b, s: (row(b, s), 0)),
            pl.BlockSpec((ts, hk), lambda b, s: (row(b, s), 1)),
            pl.BlockSpec((ts, hv), lambda b, s: (row(b, s), hk * 2 // hv)),
            pl.BlockSpec((ts, hv), lambda b, s: (row(b, s), hk * 2 // hv + 1)),
            pl.BlockSpec((ts, hk), lambda b, s: (row(b, s), 0)),
            pl.BlockSpec((1, hv), lambda b, s: (0, 0)),
        ],
        out_specs=pl.BlockSpec((ts, hv), lambda b, s: (row(b, s), 0)),
        out_shape=jax.ShapeDtypeStruct((batch * seq, hv), BF16),
        scratch_shapes=[pltpu.VMEM((GLA_HEADS, hv // GLA_HEADS, hk // GLA_HEADS), F32)],
        compiler_params=_params(("parallel", "arbitrary")),
        name="gla",
    )(proj, proj, proj, proj, la, onorm)


def _gelu(x):
    return 0.5 * x * (1.0 + lax.erf(x * (2.0 ** -0.5)))


def _sg_body(u_ref, v_ref, lg_ref, lb_ref, ws_ref, bs_ref, y_ref, *, groups, chunk):
    ts = u_ref.shape[0]
    gdim = u_ref.shape[1] // groups
    rr = lax.broadcasted_iota(jnp.int32, (chunk, chunk), 0)
    cc = lax.broadcasted_iota(jnp.int32, (chunk, chunk), 1)
    for g in range(groups):
        cols = slice(g * gdim, (g + 1) * gdim)
        w = jnp.where(rr >= cc, ws_ref[g], 0.0).astype(BF16)
        for c in range(ts // chunk):
            rows = slice(c * chunk, (c + 1) * chunk)
            u = _gelu(u_ref[rows, cols].astype(F32))
            v = _gelu(v_ref[rows, cols].astype(F32))
            mu = jnp.mean(v, axis=-1, keepdims=True)
            var = jnp.mean((v - mu) ** 2, axis=-1, keepdims=True)
            vn = (v - mu) * lax.rsqrt(var + EPS) * lg_ref[g:g + 1, :] + lb_ref[g:g + 1, :]
            vs = jnp.dot(w, vn.astype(BF16), preferred_element_type=F32) + bs_ref[g]
            y_ref[rows, cols] = (u * vs).astype(y_ref.dtype)


def _sgate(proj, ln_g, ln_b, w_s, b_s, *, u_block, ts=256):
    n = proj.shape[0]
    groups, gdim = ln_g.shape
    width = groups * gdim
    chunk = w_s.shape[1]
    return pl.pallas_call(
        functools.partial(_sg_body, groups=groups, chunk=chunk),
        grid=(n // ts,),
        in_specs=[
            pl.BlockSpec((ts, width), lambda i: (i, u_block)),
            pl.BlockSpec((ts, width), lambda i: (i, u_block + 1)),
            pl.BlockSpec((groups, gdim), lambda i: (0, 0)),
            pl.BlockSpec((groups, gdim), lambda i: (0, 0)),
            pl.BlockSpec((groups, chunk, chunk), lambda i: (0, 0, 0)),
            pl.BlockSpec((groups, chunk, 1), lambda i: (0, 0, 0)),
        ],
        out_specs=pl.BlockSpec((ts, width), lambda i: (i, 0)),
        out_shape=jax.ShapeDtypeStruct((n, width), BF16),
        compiler_params=_params(("parallel",)),
        name="sgate",
    )(proj, proj, ln_g, ln_b, w_s, b_s)


def _xa_body(q_ref, k_ref, v_ref, y_ref, *, heads):
    dh = q_ref.shape[1] // heads
    for h in range(heads):
        cols = slice(h * dh, (h + 1) * dh)
        s = lax.dot_general(q_ref[:, cols], k_ref[:, cols], NT_DIMS,
                            preferred_element_type=F32) * (dh ** -0.5)
        p = jnp.exp(s - jnp.max(s, axis=-1, keepdims=True))
        p = p / jnp.sum(p, axis=-1, keepdims=True)
        y_ref[:, cols] = jnp.dot(p.astype(BF16), v_ref[:, cols],
                                 preferred_element_type=F32).astype(y_ref.dtype)


def _xattn(proj, kv, *, q_block, batch, seq, ts=512):
    mem_len = kv.shape[0] // batch
    width = kv.shape[1] // 2
    ns = seq // ts
    return pl.pallas_call(
        functools.partial(_xa_body, heads=XA_HEADS),
        grid=(batch, ns),
        in_specs=[
            pl.BlockSpec((ts, width), lambda b, s: (b * ns + s, q_block)),
            pl.BlockSpec((mem_len, width), lambda b, s: (b, 0)),
            pl.BlockSpec((mem_len, width), lambda b, s: (b, 1)),
        ],
        out_specs=pl.BlockSpec((ts, width), lambda b, s: (b * ns + s, 0)),
        out_shape=jax.ShapeDtypeStruct((batch * seq, width), BF16),
        compiler_params=_params(("parallel", "parallel")),
        name="xattn",
    )(proj, kv, kv)


def _merge_body(ya_ref, yb_ref, yc_ref, ga_ref, gb_ref, gc_ref, wb_ref, wo_ref,
                x_ref, o_ref):
    merged = None
    for i, (y_ref, g_ref) in enumerate(((ya_ref, ga_ref), (yb_ref, gb_ref), (yc_ref, gc_ref))):
        t = jnp.dot(y_ref[...], wb_ref[i], preferred_element_type=F32)
        t = jax.nn.sigmoid(g_ref[...].astype(F32)) * t
        merged = t if merged is None else merged + t
    o_ref[...] = x_ref[...] + jnp.dot(merged.astype(BF16), wo_ref[...],
                                      preferred_element_type=F32)


def _merge(ya, yb, yc, proj, wb, wo, x, *, gate_block, tm=256):
    n, d = x.shape
    bw = ya.shape[1]
    row = lambda i: (i, 0)
    once = pl.Buffered(1)
    return pl.pallas_call(
        _merge_body,
        grid=(n // tm,),
        in_specs=[
            pl.BlockSpec((tm, bw), row), pl.BlockSpec((tm, bw), row), pl.BlockSpec((tm, bw), row),
            pl.BlockSpec((tm, d), lambda i: (i, gate_block)),
            pl.BlockSpec((tm, d), lambda i: (i, gate_block + 1)),
            pl.BlockSpec((tm, d), lambda i: (i, gate_block + 2)),
            pl.BlockSpec(wb.shape, lambda i: (0, 0, 0), pipeline_mode=once),
            pl.BlockSpec(wo.shape, lambda i: (0, 0), pipeline_mode=once),
            pl.BlockSpec((tm, d), row),
        ],
        out_specs=pl.BlockSpec((tm, d), row),
        out_shape=jax.ShapeDtypeStruct((n, d), F32),
        compiler_params=_params(("parallel",)),
        name="merge",
    )(ya, yb, yc, proj, proj, proj, wb, wo, x)


def kernel(x, mem, ffn1_norm, ffn1_w_gate, ffn1_w_up, ffn1_w_down, mix_norm, mem_norm, w_in, gla_w_gate_up, gla_gate_bias, gla_out_norm, sg_ln_g, sg_ln_b, sg_w_s, sg_b_s, w_kv_mem, w_branch, w_out, ffn2_norm, ffn2_w_gate, ffn2_w_up, ffn2_w_down, final_norm):
    batch, seq, d = x.shape
    depth = w_in.shape[0]
    hk = gla_w_gate_up.shape[2]
    hv = gla_out_norm.shape[1] * gla_out_norm.shape[2]
    bw = d // 2
    rank = gla_w_gate_up.shape[1]
    lr_lo = 2 * hk + 2 * hv

    xf = x.reshape(batch * seq, d)
    memf = mem.reshape(-1, d)
    for l in range(depth):
        last = l == depth - 1
        x1, h = _ffn(xf, ffn1_norm[l][None], ffn1_w_gate[l].astype(BF16),
                     ffn1_w_up[l].astype(BF16), ffn1_w_down[l].astype(BF16),
                     mix_norm[l][None], emit_resid=True)

        w_main = _regroup(w_in[l], cut_lo=lr_lo, cut_hi=lr_lo + rank)
        proj = _matmul(h, w_main, BF16, tm=1024, tn=1024, name="proj")
        w_lr = jnp.pad(w_in[l][:, lr_lo:lr_lo + rank], ((0, 0), (0, LANES - rank)))
        w_up = jnp.pad(gla_w_gate_up[l], ((0, LANES - rank), (0, 0)))
        w_la = _matmul(w_lr, w_up, BF16, tm=d, tn=hk, name="fold_gate", precision=HIGHEST)
        la = _la_proj(h, w_la, gla_gate_bias[l][None])
        kv = _norm_matmul(memf, mem_norm[l][None], w_kv_mem[l].astype(BF16), BF16,
                          tm=512, name="mem_kv")

        y_gla = _gla(proj, la, gla_out_norm[l].reshape(1, hv), batch=batch, seq=seq)
        y_sg = _sgate(proj, sg_ln_g[l], sg_ln_b[l], sg_w_s[l], sg_b_s[l][:, :, None],
                      u_block=lr_lo // bw)
        y_xa = _xattn(proj, kv, q_block=lr_lo // bw + 2, batch=batch, seq=seq)
        x2 = _merge(y_gla, y_sg, y_xa, proj, w_branch[l].astype(BF16), w_out[l].astype(BF16),
                    x1, gate_block=(lr_lo + 3 * bw) // d)

        post = final_norm[None] if last else ffn1_norm[l + 1][None]
        res = _ffn(x2, ffn2_norm[l][None], ffn2_w_gate[l].astype(BF16),
                   ffn2_w_up[l].astype(BF16), ffn2_w_down[l].astype(BF16),
                   post, emit_resid=not last)
        xf = res[0]
    return xf.reshape(batch, seq, d)
```

```python
import functools

import jax
import jax.numpy as jnp
from jax import lax
from jax.experimental import pallas as pl
from jax.experimental.pallas import tpu as pltpu

F32 = jnp.float32
BF16 = jnp.bfloat16
HIGHEST = lax.Precision.HIGHEST
NT_DIMS = (((1,), (1,)), ((), ()))
TN_DIMS = (((0,), (0,)), ((), ()))

EPS = 1e-6
LOG2_E = 1.4426950408889634
GLA_HEADS = 4
GLA_TAU = 16.0
GLA_CHUNK = 128
GLA_SUB = 8
XA_HEADS = 4

LANES = 128
VMEM_LIMIT = 56 * 1024 * 1024


def _params(semantics):
    return pltpu.CompilerParams(dimension_semantics=semantics,
                                vmem_limit_bytes=VMEM_LIMIT)


def _rms(x, g):
    return x * lax.rsqrt(jnp.mean(x * x, axis=-1, keepdims=True) + EPS) * g


def _ffn_body(x_ref, g_ref, wg_ref, wu_ref, wd_ref, pg_ref, *refs, emit_resid):
    if emit_resid:
        xo_ref, no_ref, xn_ref, acc_ref = refs
    else:
        no_ref, xn_ref, acc_ref = refs
    f = pl.program_id(1)

    @pl.when(f == 0)
    def _():
        xn_ref[...] = _rms(x_ref[...], g_ref[...]).astype(BF16)
        acc_ref[...] = jnp.zeros_like(acc_ref)

    xn = xn_ref[...]
    gate = jnp.dot(xn, wg_ref[...], preferred_element_type=F32)
    up = jnp.dot(xn, wu_ref[...], preferred_element_type=F32)
    hid = (jax.nn.silu(gate) * up).astype(BF16)
    acc_ref[...] += jnp.dot(hid, wd_ref[...], preferred_element_type=F32)

    @pl.when(f == pl.num_programs(1) - 1)
    def _():
        y = x_ref[...] + 0.5 * acc_ref[...]
        if emit_resid:
            xo_ref[...] = y
        no_ref[...] = _rms(y, pg_ref[...]).astype(no_ref.dtype)


def _ffn(x, g, wg, wu, wd, post_g, *, emit_resid, tm=512, tf=512):
    n, d = x.shape
    dff = wg.shape[1]
    row = lambda i, f: (i, 0)
    out_shape = [jax.ShapeDtypeStruct((n, d), BF16 if emit_resid else F32)]
    out_specs = [pl.BlockSpec((tm, d), row)]
    if emit_resid:
        out_shape.insert(0, jax.ShapeDtypeStruct((n, d), F32))
        out_specs.insert(0, pl.BlockSpec((tm, d), row))
    return pl.pallas_call(
        functools.partial(_ffn_body, emit_resid=emit_resid),
        grid=(n // tm, dff // tf),
        in_specs=[
            pl.BlockSpec((tm, d), row),
            pl.BlockSpec((1, d), lambda i, f: (0, 0)),
            pl.BlockSpec((d, tf), lambda i, f: (0, f)),
            pl.BlockSpec((d, tf), lambda i, f: (0, f)),
            pl.BlockSpec((tf, d), lambda i, f: (f, 0)),
            pl.BlockSpec((1, d), lambda i, f: (0, 0)),
        ],
        out_specs=out_specs,
        out_shape=out_shape,
        scratch_shapes=[pltpu.VMEM((tm, d), BF16), pltpu.VMEM((tm, d), F32)],
        compiler_params=_params(("parallel", "arbitrary")),
        name="ffn",
    )(x, g, wg, wu, wd, post_g)


def _regroup_body(a_ref, b_ref, o_ref, cut_ref, *, cut_blocks, gap):
    c = pl.program_id(0)

    @pl.when(c < cut_blocks)
    def _():
        o_ref[...] = a_ref[...].T.astype(BF16)

    @pl.when(c == cut_blocks - 1)
    def _():
        pad = jnp.zeros((cut_ref.shape[1] - gap, b_ref.shape[1]), F32)
        cut_ref[...] = jnp.concatenate([b_ref[...], pad], axis=0).T

    @pl.when(c >= cut_blocks)
    def _():
        rows = jnp.concatenate([a_ref[gap:, :], b_ref[...]], axis=0)
        o_ref[...] = rows.T.astype(BF16)


def _regroup(wt, *, cut_lo, gap, tc=512):
    width, d = wt.shape
    out_w = width - gap
    assert cut_lo % tc == 0 and cut_lo >= tc and out_w % tc == 0 and tc % gap == 0
    return pl.pallas_call(
        functools.partial(_regroup_body, cut_blocks=cut_lo // tc, gap=gap),
        grid=(out_w // tc,),
        in_specs=[pl.BlockSpec((tc, d), lambda c: (c, 0)),
                  pl.BlockSpec((gap, d), lambda c: ((c + 1) * (tc // gap), 0))],
        out_specs=[pl.BlockSpec((d, tc), lambda c: (0, c)),
                   pl.BlockSpec((d, LANES), lambda c: (0, 0))],
        out_shape=[jax.ShapeDtypeStruct((d, out_w), BF16),
                   jax.ShapeDtypeStruct((d, LANES), F32)],
        compiler_params=_params(("arbitrary",)),
        name="regroup",
    )(wt, wt)


def _matmul_body(x_ref, w_ref, o_ref, *, precision):
    o_ref[...] = jnp.dot(x_ref[...], w_ref[...], preferred_element_type=F32,
                         precision=precision).astype(o_ref.dtype)


def _matmul(x, w, out_dtype, *, tm, tn, name, precision=None):
    n, d = x.shape
    nc = w.shape[1]
    return pl.pallas_call(
        functools.partial(_matmul_body, precision=precision),
        grid=(n // tm, nc // tn),
        in_specs=[pl.BlockSpec((tm, d), lambda i, j: (i, 0)),
                  pl.BlockSpec((d, tn), lambda i, j: (0, j))],
        out_specs=pl.BlockSpec((tm, tn), lambda i, j: (i, j)),
        out_shape=jax.ShapeDtypeStruct((n, nc), out_dtype),
        compiler_params=_params(("parallel", "parallel")),
        name=name,
    )(x, w)


def _norm_matmul_body(x_ref, g_ref, w_ref, o_ref):
    xn = _rms(x_ref[...], g_ref[...]).astype(BF16)
    o_ref[...] = jnp.dot(xn, w_ref[...], preferred_element_type=F32).astype(o_ref.dtype)


def _norm_matmul(x, g, w, out_dtype, *, tm, name):
    n, d = x.shape
    nc = w.shape[1]
    return pl.pallas_call(
        _norm_matmul_body,
        grid=(n // tm,),
        in_specs=[pl.BlockSpec((tm, d), lambda i: (i, 0)),
                  pl.BlockSpec((1, d), lambda i: (0, 0)),
                  pl.BlockSpec((d, nc), lambda i: (0, 0))],
        out_specs=pl.BlockSpec((tm, nc), lambda i: (i, 0)),
        out_shape=jax.ShapeDtypeStruct((n, nc), out_dtype),
        compiler_params=_params(("parallel",)),
        name=name,
    )(x, g, w)


def _la_body(x_ref, w_ref, b_ref, o_ref):
    z = jnp.dot(x_ref[...], w_ref[...], preferred_element_type=F32) + b_ref[...]
    o_ref[...] = jax.nn.log_sigmoid(z) * (1.0 / GLA_TAU)


def _la_proj(x, w, bias, *, tm=1024):
    n, d = x.shape
    nc = w.shape[1]
    return pl.pallas_call(
        _la_body,
        grid=(n // tm,),
        in_specs=[pl.BlockSpec((tm, d), lambda i: (i, 0)),
                  pl.BlockSpec((d, nc), lambda i: (0, 0)),
                  pl.BlockSpec((1, nc), lambda i: (0, 0))],
        out_specs=pl.BlockSpec((tm, nc), lambda i: (i, 0)),
        out_shape=jax.ShapeDtypeStruct((n, nc), F32),
        compiler_params=_params(("parallel",)),
        name="la_proj",
    )(x, w, bias)


def _gla_body(q_ref, k_ref, v_ref, r_ref, la_ref, on_ref, y_ref, st_ref, kb_ref,
              *, heads, chunk, sub):
    ts = q_ref.shape[0]
    dk = q_ref.shape[1] // heads
    dv = v_ref.shape[1] // heads
    scale = dk ** -0.5

    @pl.when(pl.program_id(1) == 0)
    def _():
        st_ref[...] = jnp.zeros_like(st_ref)

    t_idx = lax.broadcasted_iota(jnp.int32, (chunk, chunk), 0)
    s_idx = lax.broadcasted_iota(jnp.int32, (chunk, chunk), 1)
    causal = t_idx >= s_idx
    ones_tril = causal.astype(BF16)
    lane_in_sub = lax.broadcasted_iota(jnp.int32, (sub, chunk), 1) % sub

    halves = []
    m = chunk // 2
    while m >= sub:
        halves.append(m)
        m //= 2
    lane_half = {m: lax.broadcasted_iota(jnp.int32, (m, chunk), 1) >> (m.bit_length() - 1)
                 for m in halves}

    def chunk_step(c, carry):
        r0 = pl.multiple_of(c * chunk, chunk)
        rows = pl.ds(r0, chunk)
        la = la_ref[rows, :] * LOG2_E
        la_hi = la.astype(BF16)
        la_lo = (la - la_hi.astype(F32)).astype(BF16)
        b_all = (jnp.dot(ones_tril, la_hi, preferred_element_type=F32)
                 + jnp.dot(ones_tril, la_lo, preferred_element_type=F32))
        for h in range(heads):
            kcols = slice(h * dk, (h + 1) * dk)
            vcols = slice(h * dv, (h + 1) * dv)
            b = b_all[:, kcols]
            q = q_ref[rows, kcols].astype(F32) * scale
            k = k_ref[rows, kcols].astype(F32)
            v = v_ref[rows, vcols]
            st = st_ref[h]
            inter = lax.dot_general((q * jnp.exp2(b)).astype(BF16), st.astype(BF16),
                                    NT_DIMS, preferred_element_type=F32)

            kb_ref[h, 0] = k
            kb_ref[h, 1] = b
            diag = []
            for i in range(chunk // sub):
                lo = i * sub
                qi, bi = q[lo:lo + sub], b[lo:lo + sub]
                d = jnp.zeros((sub, chunk), F32)
                for s in range(sub):
                    ks = jnp.broadcast_to(kb_ref[h, 0, lo + s:lo + s + 1, :], (sub, dk))
                    bs = jnp.broadcast_to(kb_ref[h, 1, lo + s:lo + s + 1, :], (sub, dk))
                    col = jnp.sum(qi * ks * jnp.exp2(bi - bs), axis=-1, keepdims=True)
                    d = jnp.where(lane_in_sub == s, col, d)
                diag.append(d)
            scores = jnp.concatenate(diag, axis=0)

            for m in halves:
                parts = []
                for j in range(0, chunk, 2 * m):
                    b_mid = b[j + m - 1:j + m]
                    parts.append(k[j:j + m] * jnp.exp2(b_mid - b[j:j + m]))
                    parts.append(q[j + m:j + 2 * m] * jnp.exp2(b[j + m:j + 2 * m] - b_mid))
                z = jnp.concatenate(parts, axis=0).astype(BF16)
                pair = lax.dot_general(z, z, NT_DIMS, preferred_element_type=F32)
                out = []
                for j in range(0, chunk, 2 * m):
                    out.append(scores[j:j + m])
                    out.append(jnp.where(lane_half[m] == j // m, pair[j + m:j + 2 * m],
                                         scores[j + m:j + 2 * m]))
                scores = jnp.concatenate(out, axis=0)
            scores = jnp.where(causal, scores, 0.0)
            o = inter + jnp.dot(scores.astype(BF16), v, preferred_element_type=F32)

            b_end = b[chunk - 1:chunk]
            k_end = (k * jnp.exp2(b_end - b)).astype(BF16)
            upd = lax.dot_general(v, k_end, TN_DIMS, preferred_element_type=F32)
            st_ref[h] = st * jnp.exp2(b_end) + upd

            o = o * lax.rsqrt(jnp.mean(o * o, axis=-1, keepdims=True) + EPS)
            o = o * on_ref[:, vcols] * jax.nn.silu(r_ref[rows, vcols].astype(F32))
            y_ref[rows, vcols] = o.astype(y_ref.dtype)
        return carry

    lax.fori_loop(0, ts // chunk, chunk_step, 0)


def _gla(proj, la, onorm, *, batch, seq, ts=512):
    hk = la.shape[1]
    hv = onorm.shape[1]
    ns = seq // ts
    row = lambda b, s: b * ns + s
    return pl.pallas_call(
        functools.partial(_gla_body, heads=GLA_HEADS, chunk=GLA_CHUNK, sub=GLA_SUB),
        grid=(batch, ns),
        in_specs=[
            pl.BlockSpec((ts, hk), lambda b, s: (row(b, s), 0)),
            pl.BlockSpec((ts, hk), lambda b, s: (row(b, s), 1)),
            pl.BlockSpec((ts, hv), lambda b, s: (row(b, s), hk * 2 // hv)),
            pl.BlockSpec((ts, hv), lambda b, s: (row(b, s), hk * 2 // hv + 1)),
            pl.BlockSpec((ts, hk), lambda b, s: (row(b, s), 0)),
            pl.BlockSpec((1, hv), lambda b, s: (0, 0)),
        ],
        out_specs=pl.BlockSpec((ts, hv), lambda b, s: (row(b, s), 0)),
        out_shape=jax.ShapeDtypeStruct((batch * seq, hv), BF16),
        scratch_shapes=[pltpu.VMEM((GLA_HEADS, hv // GLA_HEADS, hk // GLA_HEADS), F32),
                        pltpu.VMEM((GLA_HEADS, 2, GLA_CHUNK, hk // GLA_HEADS), F32)],
        compiler_params=_params(("parallel", "arbitrary")),
        name="gla",
    )(proj, proj, proj, proj, la, onorm)


def _gelu(x):
    return 0.5 * x * (1.0 + lax.erf(x * (2.0 ** -0.5)))


def _sg_body(u_ref, v_ref, lg_ref, lb_ref, ws_ref, bs_ref, y_ref, *, groups, chunk):
    ts = u_ref.shape[0]
    gdim = u_ref.shape[1] // groups
    rr = lax.broadcasted_iota(jnp.int32, (chunk, chunk), 0)
    cc = lax.broadcasted_iota(jnp.int32, (chunk, chunk), 1)
    for g in range(groups):
        cols = slice(g * gdim, (g + 1) * gdim)
        w = jnp.where(rr >= cc, ws_ref[g], 0.0).astype(BF16)
        for c in range(ts // chunk):
            rows = slice(c * chunk, (c + 1) * chunk)
            u = _gelu(u_ref[rows, cols].astype(F32))
            v = _gelu(v_ref[rows, cols].astype(F32))
            mu = jnp.mean(v, axis=-1, keepdims=True)
            var = jnp.mean((v - mu) ** 2, axis=-1, keepdims=True)
            vn = (v - mu) * lax.rsqrt(var + EPS) * lg_ref[g:g + 1, :] + lb_ref[g:g + 1, :]
            vs = jnp.dot(w, vn.astype(BF16), preferred_element_type=F32) + bs_ref[g]
            y_ref[rows, cols] = (u * vs).astype(y_ref.dtype)


def _xa_body(q_ref, k_ref, v_ref, y_ref, *, heads):
    dh = q_ref.shape[1] // heads
    for h in range(heads):
        cols = slice(h * dh, (h + 1) * dh)
        s = lax.dot_general(q_ref[:, cols], k_ref[:, cols], NT_DIMS,
                            preferred_element_type=F32) * (dh ** -0.5)
        p = jnp.exp(s - jnp.max(s, axis=-1, keepdims=True))
        p = p / jnp.sum(p, axis=-1, keepdims=True)
        y_ref[:, cols] = jnp.dot(p.astype(BF16), v_ref[:, cols],
                                 preferred_element_type=F32).astype(y_ref.dtype)


def _merge_body(su0_ref, sv0_ref, xq0_ref, mk0_ref, mv0_ref,
                su_ref, sv_ref, xq_ref, mk_ref, mv_ref,
                ya_ref, ga_ref, gb_ref, gc_ref, lg_ref, lb_ref, ws_ref, bs_ref,
                wb_ref, wo_ref, x_ref, o_ref,
                yb_cur, yc_cur, yb_nxt, yc_nxt, *, groups, chunk, heads):
    def branches(u_ref, v_ref, q_ref, k_ref, w_ref, yb_ref, yc_ref):
        _sg_body(u_ref, v_ref, lg_ref, lb_ref, ws_ref, bs_ref, yb_ref, groups=groups, chunk=chunk)
        _xa_body(q_ref, k_ref, w_ref, yc_ref, heads=heads)

    @pl.when(pl.program_id(0) == 0)
    def _():
        branches(su0_ref, sv0_ref, xq0_ref, mk0_ref, mv0_ref, yb_cur, yc_cur)

    branches(su_ref, sv_ref, xq_ref, mk_ref, mv_ref, yb_nxt, yc_nxt)

    merged = None
    for i, (y_ref, g_ref) in enumerate(((ya_ref, ga_ref), (yb_cur, gb_ref), (yc_cur, gc_ref))):
        t = jnp.dot(y_ref[...], wb_ref[i], preferred_element_type=F32)
        t = jax.nn.sigmoid(g_ref[...].astype(F32)) * t
        merged = t if merged is None else merged + t
    o_ref[...] = x_ref[...] + jnp.dot(merged.astype(BF16), wo_ref[...],
                                      preferred_element_type=F32)
    yb_cur[...] = yb_nxt[...]
    yc_cur[...] = yc_nxt[...]


def _merge(ya, proj, kv, ln_g, ln_b, w_s, b_s, wb, wo, x, *, u_block, q_block, gate_block,
           seq, tm=256):
    n, d = x.shape
    bw = ya.shape[1]
    groups, gdim = ln_g.shape
    chunk = w_s.shape[1]
    mem_len = kv.shape[0] // (n // seq)
    tiles_per_seq = seq // tm
    last = n // tm - 1
    assert tm % chunk == 0 and groups * gdim == bw and kv.shape[1] == 2 * bw
    row = lambda i: (i, 0)
    nxt = lambda i: jnp.minimum(i + 1, last)
    const2 = lambda i: (0, 0)
    const3 = lambda i: (0, 0, 0)
    once = pl.Buffered(1)
    tile = lambda f: pl.BlockSpec((tm, bw), f)
    mem = lambda f: pl.BlockSpec((mem_len, bw), f)
    return pl.pallas_call(
        functools.partial(_merge_body, groups=groups, chunk=chunk, heads=XA_HEADS),
        grid=(n // tm,),
        in_specs=[
            tile(lambda i: (0, u_block)), tile(lambda i: (0, u_block + 1)),
            tile(lambda i: (0, q_block)),
            mem(lambda i: (0, 0)), mem(lambda i: (0, 1)),
            tile(lambda i: (nxt(i), u_block)), tile(lambda i: (nxt(i), u_block + 1)),
            tile(lambda i: (nxt(i), q_block)),
            mem(lambda i: (nxt(i) // tiles_per_seq, 0)), mem(lambda i: (nxt(i) // tiles_per_seq, 1)),
            tile(row),
            pl.BlockSpec((tm, d), lambda i: (i, gate_block)),
            pl.BlockSpec((tm, d), lambda i: (i, gate_block + 1)),
            pl.BlockSpec((tm, d), lambda i: (i, gate_block + 2)),
            pl.BlockSpec((groups, gdim), const2),
            pl.BlockSpec((groups, gdim), const2),
            pl.BlockSpec((groups, chunk, chunk), const3),
            pl.BlockSpec((groups, chunk, 1), const3),
            pl.BlockSpec(wb.shape, const3, pipeline_mode=once),
            pl.BlockSpec(wo.shape, const2, pipeline_mode=once),
            pl.BlockSpec((tm, d), row),
        ],
        out_specs=pl.BlockSpec((tm, d), row),
        out_shape=jax.ShapeDtypeStruct((n, d), F32),
        scratch_shapes=[pltpu.VMEM((tm, bw), BF16)] * 4,
        compiler_params=_params(("arbitrary",)),
        name="merge",
    )(proj, proj, proj, kv, kv, proj, proj, proj, kv, kv, ya, proj, proj, proj,
      ln_g, ln_b, w_s, b_s, wb, wo, x)


def kernel(x, mem, ffn1_norm, ffn1_w_gate, ffn1_w_up, ffn1_w_down, mix_norm, mem_norm, w_in, gla_w_gate_up, gla_gate_bias, gla_out_norm, sg_ln_g, sg_ln_b, sg_w_s, sg_b_s, w_kv_mem, w_branch, w_out, ffn2_norm, ffn2_w_gate, ffn2_w_up, ffn2_w_down, final_norm):
    batch, seq, d = x.shape
    depth = w_in.shape[0]
    hk = gla_w_gate_up.shape[2]
    hv = gla_out_norm.shape[1] * gla_out_norm.shape[2]
    bw = d // 2
    rank = gla_w_gate_up.shape[1]
    lr_lo = 2 * hk + 2 * hv

    xf = x.reshape(batch * seq, d)
    memf = mem.reshape(-1, d)
    for l in range(depth):
        last = l == depth - 1
        x1, h = _ffn(xf, ffn1_norm[l][None], ffn1_w_gate[l].astype(BF16),
                     ffn1_w_up[l].astype(BF16), ffn1_w_down[l].astype(BF16),
                     mix_norm[l][None], emit_resid=True)

        w_in_t = jnp.swapaxes(w_in[l], 0, 1)
        w_main, w_lr = _regroup(w_in_t, cut_lo=lr_lo, gap=rank)
        proj = _matmul(h, w_main, BF16, tm=1024, tn=1024, name="proj")
        w_up = jnp.pad(gla_w_gate_up[l], ((0, LANES - rank), (0, 0)))
        w_la = _matmul(w_lr, w_up, BF16, tm=d, tn=hk, name="fold_gate", precision=HIGHEST)
        la = _la_proj(h, w_la, gla_gate_bias[l][None])
        kv = _norm_matmul(memf, mem_norm[l][None], w_kv_mem[l].astype(BF16), BF16,
                          tm=512, name="mem_kv")

        y_gla = _gla(proj, la, gla_out_norm[l].reshape(1, hv), batch=batch, seq=seq)
        x2 = _merge(y_gla, proj, kv, sg_ln_g[l], sg_ln_b[l], sg_w_s[l], sg_b_s[l][:, :, None],
                    w_branch[l].astype(BF16), w_out[l].astype(BF16), x1,
                    u_block=lr_lo // bw, q_block=lr_lo // bw + 2,
                    gate_block=(lr_lo + 3 * bw) // d, seq=seq)

        post = final_norm[None] if last else ffn1_norm[l + 1][None]
        res = _ffn(x2, ffn2_norm[l][None], ffn2_w_gate[l].astype(BF16),
                   ffn2_w_up[l].astype(BF16), ffn2_w_down[l].astype(BF16),
                   post, emit_resid=not last)
        xf = res[0]
    return xf.reshape(batch, seq, d)
```

```python
import functools

import jax
import jax.numpy as jnp
from jax import lax
from jax.experimental import pallas as pl
from jax.experimental.pallas import tpu as pltpu

F32 = jnp.float32
BF16 = jnp.bfloat16
HIGHEST = lax.Precision.HIGHEST
NT_DIMS = (((1,), (1,)), ((), ()))
TN_DIMS = (((0,), (0,)), ((), ()))

EPS = 1e-6
LOG2_E = 1.4426950408889634
GLA_HEADS = 4
GLA_TAU = 16.0
GLA_CHUNK = 128
GLA_SUB = 8
XA_HEADS = 4

LANES = 128
VMEM_LIMIT = 60 * 1024 * 1024


def _params(semantics):
    return pltpu.CompilerParams(dimension_semantics=semantics,
                                vmem_limit_bytes=VMEM_LIMIT)


def _rms(x, g):
    return x * lax.rsqrt(jnp.mean(x * x, axis=-1, keepdims=True) + EPS) * g


def _ffn_body(x_ref, g_ref, wg_ref, wu_ref, wd_ref, *refs, post_norm, n_casts, half):
    refs = list(refs)
    pg_ref = refs.pop(0) if post_norm else None
    cast_in = [refs.pop(0) for _ in range(n_casts)]
    o_ref = refs.pop(0)
    cast_out = [refs.pop(0) for _ in range(n_casts)]
    (xn_ref,) = refs
    f = pl.program_id(1)

    @pl.when(f == 0)
    def _():
        xn_ref[...] = _rms(x_ref[...], g_ref[...]).astype(BF16)
        o_ref[...] = jnp.zeros_like(o_ref)

    for src_ref, dst_ref in zip(cast_in, cast_out):
        dst_ref[...] = src_ref[...].astype(BF16)

    for r in range(0, x_ref.shape[0], half):
        rows = slice(r, r + half)
        xn = xn_ref[rows, :]
        gate = jnp.dot(xn, wg_ref[...], preferred_element_type=F32)
        up = jnp.dot(xn, wu_ref[...], preferred_element_type=F32)
        hid = (jax.nn.silu(gate) * up).astype(BF16)
        o_ref[rows, :] += jnp.dot(hid, wd_ref[...], preferred_element_type=F32)

    @pl.when(f == pl.num_programs(1) - 1)
    def _():
        y = x_ref[...] + 0.5 * o_ref[...]
        o_ref[...] = _rms(y, pg_ref[...]) if post_norm else y


def _ffn(x, g, wg, wu, wd, post_g=None, casts=(), *, tm=1024, tf=512, half=512):
    n, d = x.shape
    dff = wg.shape[1]
    ni, nf = n // tm, dff // tf
    row = lambda i, f: (i, 0)
    vec = pl.BlockSpec((1, d), lambda i, f: (0, 0))
    in_specs = [pl.BlockSpec((tm, d), row), vec,
                pl.BlockSpec((d, tf), lambda i, f: (0, f)),
                pl.BlockSpec((d, tf), lambda i, f: (0, f)),
                pl.BlockSpec((tf, d), lambda i, f: (f, 0))]
    args = [x, g, wg, wu, wd]
    if post_g is not None:
        in_specs.append(vec)
        args.append(post_g)
    cast_specs = []
    for a in casts:
        rws, cls = a.shape
        if cls % nf == 0 and (cls // nf) % LANES == 0 and rws % ni == 0:
            cast_specs.append(pl.BlockSpec((rws // ni, cls // nf), lambda i, f: (i, f)))
        else:
            assert rws % (ni * nf) == 0
            cast_specs.append(pl.BlockSpec((rws // (ni * nf), cls),
                                           lambda i, f: (i * nf + f, 0)))
    return pl.pallas_call(
        functools.partial(_ffn_body, post_norm=post_g is not None, n_casts=len(casts),
                          half=half),
        grid=(ni, nf),
        in_specs=in_specs + cast_specs,
        out_specs=[pl.BlockSpec((tm, d), row)] + cast_specs,
        out_shape=[jax.ShapeDtypeStruct((n, d), F32)]
                  + [jax.ShapeDtypeStruct(a.shape, BF16) for a in casts],
        scratch_shapes=[pltpu.VMEM((tm, d), BF16)],
        compiler_params=_params(("parallel", "arbitrary")),
        name="ffn",
    )(*args, *casts)


def _regroup_body(a_ref, b_ref, o_ref, cut_ref, *, cut_blocks, gap):
    c = pl.program_id(0)

    @pl.when(c < cut_blocks)
    def _():
        o_ref[...] = a_ref[...].T.astype(BF16)

    @pl.when(c == cut_blocks - 1)
    def _():
        pad = jnp.zeros((cut_ref.shape[1] - gap, b_ref.shape[1]), F32)
        cut_ref[...] = jnp.concatenate([b_ref[...], pad], axis=0).T

    @pl.when(c >= cut_blocks)
    def _():
        rows = jnp.concatenate([a_ref[gap:, :], b_ref[...]], axis=0)
        o_ref[...] = rows.T.astype(BF16)


def _regroup(wt, *, cut_lo, gap, tc=512):
    width, d = wt.shape
    out_w = width - gap
    assert cut_lo % tc == 0 and cut_lo >= tc and out_w % tc == 0 and tc % gap == 0
    return pl.pallas_call(
        functools.partial(_regroup_body, cut_blocks=cut_lo // tc, gap=gap),
        grid=(out_w // tc,),
        in_specs=[pl.BlockSpec((tc, d), lambda c: (c, 0)),
                  pl.BlockSpec((gap, d), lambda c: ((c + 1) * (tc // gap), 0))],
        out_specs=[pl.BlockSpec((d, tc), lambda c: (0, c)),
                   pl.BlockSpec((d, LANES), lambda c: (0, 0))],
        out_shape=[jax.ShapeDtypeStruct((d, out_w), BF16),
                   jax.ShapeDtypeStruct((d, LANES), F32)],
        compiler_params=_params(("arbitrary",)),
        name="regroup",
    )(wt, wt)


def _matmul_body(x_ref, w_ref, o_ref, *, precision):
    o_ref[...] = jnp.dot(x_ref[...], w_ref[...], preferred_element_type=F32,
                         precision=precision).astype(o_ref.dtype)


def _matmul(x, w, out_dtype, *, tm, tn, name, precision=None):
    n, d = x.shape
    nc = w.shape[1]
    return pl.pallas_call(
        functools.partial(_matmul_body, precision=precision),
        grid=(n // tm, nc // tn),
        in_specs=[pl.BlockSpec((tm, d), lambda i, j: (i, 0)),
                  pl.BlockSpec((d, tn), lambda i, j: (0, j))],
        out_specs=pl.BlockSpec((tm, tn), lambda i, j: (i, j)),
        out_shape=jax.ShapeDtypeStruct((n, nc), out_dtype),
        compiler_params=_params(("parallel", "parallel")),
        name=name,
    )(x, w)


def _proj_body(x_ref, g_ref, w_ref, o_ref, h_ref):
    @pl.when(pl.program_id(1) == 0)
    def _():
        h_ref[...] = _rms(x_ref[...], g_ref[...]).astype(BF16)

    o_ref[...] = jnp.dot(h_ref[...], w_ref[...],
                         preferred_element_type=F32).astype(o_ref.dtype)


def _proj(x, g, w, *, tm=1024, tn=1024):
    n, d = x.shape
    nc = w.shape[1]
    return pl.pallas_call(
        _proj_body,
        grid=(n // tm, nc // tn),
        in_specs=[pl.BlockSpec((tm, d), lambda i, j: (i, 0)),
                  pl.BlockSpec((1, d), lambda i, j: (0, 0)),
                  pl.BlockSpec((d, tn), lambda i, j: (0, j))],
        out_specs=[pl.BlockSpec((tm, tn), lambda i, j: (i, j)),
                   pl.BlockSpec((tm, d), lambda i, j: (i, 0))],
        out_shape=[jax.ShapeDtypeStruct((n, nc), BF16),
                   jax.ShapeDtypeStruct((n, d), BF16)],
        compiler_params=_params(("parallel", "arbitrary")),
        name="proj",
    )(x, g, w)


def _norm_matmul_body(x_ref, g_ref, w_ref, o_ref):
    xn = _rms(x_ref[...], g_ref[...]).astype(BF16)
    o_ref[...] = jnp.dot(xn, w_ref[...], preferred_element_type=F32).astype(o_ref.dtype)


def _norm_matmul(x, g, w, out_dtype, *, tm, name):
    n, d = x.shape
    nc = w.shape[1]
    return pl.pallas_call(
        _norm_matmul_body,
        grid=(n // tm,),
        in_specs=[pl.BlockSpec((tm, d), lambda i: (i, 0)),
                  pl.BlockSpec((1, d), lambda i: (0, 0)),
                  pl.BlockSpec((d, nc), lambda i: (0, 0))],
        out_specs=pl.BlockSpec((tm, nc), lambda i: (i, 0)),
        out_shape=jax.ShapeDtypeStruct((n, nc), out_dtype),
        compiler_params=_params(("parallel",)),
        name=name,
    )(x, g, w)


def _la_body(x_ref, w_ref, b_ref, o_ref):
    z = jnp.dot(x_ref[...], w_ref[...], preferred_element_type=F32) + b_ref[...]
    o_ref[...] = jax.nn.log_sigmoid(z) * (1.0 / GLA_TAU)


def _la_proj(x, w, bias, *, tm=1024):
    n, d = x.shape
    nc = w.shape[1]
    return pl.pallas_call(
        _la_body,
        grid=(n // tm,),
        in_specs=[pl.BlockSpec((tm, d), lambda i: (i, 0)),
                  pl.BlockSpec((d, nc), lambda i: (0, 0)),
                  pl.BlockSpec((1, nc), lambda i: (0, 0))],
        out_specs=pl.BlockSpec((tm, nc), lambda i: (i, 0)),
        out_shape=jax.ShapeDtypeStruct((n, nc), F32),
        compiler_params=_params(("parallel",)),
        name="la_proj",
    )(x, w, bias)


def _gla_body(q_ref, k_ref, v_ref, r_ref, la_ref, on_ref, y_ref, st_ref, kb_ref,
              *, heads, chunk, sub):
    ts = q_ref.shape[0]
    dk = q_ref.shape[1] // heads
    dv = v_ref.shape[1] // heads
    scale = dk ** -0.5

    @pl.when(pl.program_id(1) == 0)
    def _():
        st_ref[...] = jnp.zeros_like(st_ref)

    t_idx = lax.broadcasted_iota(jnp.int32, (chunk, chunk), 0)
    s_idx = lax.broadcasted_iota(jnp.int32, (chunk, chunk), 1)
    causal = t_idx >= s_idx
    ones_tril = causal.astype(BF16)
    lane_in_sub = lax.broadcasted_iota(jnp.int32, (sub, chunk), 1) % sub

    halves = []
    m = chunk // 2
    while m >= sub:
        halves.append(m)
        m //= 2
    lane_half = {m: lax.broadcasted_iota(jnp.int32, (m, chunk), 1) >> (m.bit_length() - 1)
                 for m in halves}

    def chunk_step(c, carry):
        r0 = pl.multiple_of(c * chunk, chunk)
        rows = pl.ds(r0, chunk)
        la = la_ref[rows, :] * LOG2_E
        la_hi = la.astype(BF16)
        la_lo = (la - la_hi.astype(F32)).astype(BF16)
        b_all = (jnp.dot(ones_tril, la_hi, preferred_element_type=F32)
                 + jnp.dot(ones_tril, la_lo, preferred_element_type=F32))
        for h in range(heads):
            kcols = slice(h * dk, (h + 1) * dk)
            vcols = slice(h * dv, (h + 1) * dv)
            b = b_all[:, kcols]
            q = q_ref[rows, kcols].astype(F32) * scale
            k = k_ref[rows, kcols].astype(F32)
            v = v_ref[rows, vcols]
            st = st_ref[h]
            inter = lax.dot_general((q * jnp.exp2(b)).astype(BF16), st.astype(BF16),
                                    NT_DIMS, preferred_element_type=F32)

            kb_ref[h, 0] = k
            kb_ref[h, 1] = b
            diag = []
            for i in range(chunk // sub):
                lo = i * sub
                qi, bi = q[lo:lo + sub], b[lo:lo + sub]
                d = jnp.zeros((sub, chunk), F32)
                for s in range(sub):
                    ks = jnp.broadcast_to(kb_ref[h, 0, lo + s:lo + s + 1, :], (sub, dk))
                    bs = jnp.broadcast_to(kb_ref[h, 1, lo + s:lo + s + 1, :], (sub, dk))
                    col = jnp.sum(qi * ks * jnp.exp2(bi - bs), axis=-1, keepdims=True)
                    d = jnp.where(lane_in_sub == s, col, d)
                diag.append(d)
            scores = jnp.concatenate(diag, axis=0)

            for m in halves:
                parts = []
                for j in range(0, chunk, 2 * m):
                    b_mid = b[j + m - 1:j + m]
                    parts.append(k[j:j + m] * jnp.exp2(b_mid - b[j:j + m]))
                    parts.append(q[j + m:j + 2 * m] * jnp.exp2(b[j + m:j + 2 * m] - b_mid))
                z = jnp.concatenate(parts, axis=0).astype(BF16)
                pair = lax.dot_general(z, z, NT_DIMS, preferred_element_type=F32)
                out = []
                for j in range(0, chunk, 2 * m):
                    out.append(scores[j:j + m])
                    out.append(jnp.where(lane_half[m] == j // m, pair[j + m:j + 2 * m],
                                         scores[j + m:j + 2 * m]))
                scores = jnp.concatenate(out, axis=0)
            scores = jnp.where(causal, scores, 0.0)
            o = inter + jnp.dot(scores.astype(BF16), v, preferred_element_type=F32)

            b_end = b[chunk - 1:chunk]
            k_end = (k * jnp.exp2(b_end - b)).astype(BF16)
            upd = lax.dot_general(v, k_end, TN_DIMS, preferred_element_type=F32)
            st_ref[h] = st * jnp.exp2(b_end) + upd

            o = o * lax.rsqrt(jnp.mean(o * o, axis=-1, keepdims=True) + EPS)
            o = o * on_ref[:, vcols] * jax.nn.silu(r_ref[rows, vcols].astype(F32))
            y_ref[rows, vcols] = o.astype(y_ref.dtype)
        return carry

    lax.fori_loop(0, ts // chunk, chunk_step, 0)


def _gla(proj, la, onorm, *, batch, seq, ts=512):
    hk = la.shape[1]
    hv = onorm.shape[1]
    ns = seq // ts
    row = lambda b, s: b * ns + s
    return pl.pallas_call(
        functools.partial(_gla_body, heads=GLA_HEADS, chunk=GLA_CHUNK, sub=GLA_SUB),
        grid=(batch, ns),
        in_specs=[
            pl.BlockSpec((ts, hk), lambda b, s: (row(b, s), 0)),
            pl.BlockSpec((ts, hk), lambda b, s: (row(b, s), 1)),
            pl.BlockSpec((ts, hv), lambda b, s: (row(b, s), hk * 2 // hv)),
            pl.BlockSpec((ts, hv), lambda b, s: (row(b, s), hk * 2 // hv + 1)),
            pl.BlockSpec((ts, hk), lambda b, s: (row(b, s), 0)),
            pl.BlockSpec((1, hv), lambda b, s: (0, 0)),
        ],
        out_specs=pl.BlockSpec((ts, hv), lambda b, s: (row(b, s), 0)),
        out_shape=jax.ShapeDtypeStruct((batch * seq, hv), BF16),
        scratch_shapes=[pltpu.VMEM((GLA_HEADS, hv // GLA_HEADS, hk // GLA_HEADS), F32),
                        pltpu.VMEM((GLA_HEADS, 2, GLA_CHUNK, hk // GLA_HEADS), F32)],
        compiler_params=_params(("parallel", "arbitrary")),
        name="gla",
    )(proj, proj, proj, proj, la, onorm)


def _gelu(x):
    return 0.5 * x * (1.0 + lax.erf(x * (2.0 ** -0.5)))


def _sg_body(u_ref, v_ref, lg_ref, lb_ref, ws_ref, bs_ref, y_ref, *, groups, chunk):
    ts = u_ref.shape[0]
    gdim = u_ref.shape[1] // groups
    rr = lax.broadcasted_iota(jnp.int32, (chunk, chunk), 0)
    cc = lax.broadcasted_iota(jnp.int32, (chunk, chunk), 1)
    for g in range(groups):
        cols = slice(g * gdim, (g + 1) * gdim)
        w = jnp.where(rr >= cc, ws_ref[g], 0.0).astype(BF16)
        for c in range(ts // chunk):
            rows = slice(c * chunk, (c + 1) * chunk)
            u = _gelu(u_ref[rows, cols].astype(F32))
            v = _gelu(v_ref[rows, cols].astype(F32))
            mu = jnp.mean(v, axis=-1, keepdims=True)
            var = jnp.mean((v - mu) ** 2, axis=-1, keepdims=True)
            vn = (v - mu) * lax.rsqrt(var + EPS) * lg_ref[g:g + 1, :] + lb_ref[g:g + 1, :]
            vs = jnp.dot(w, vn.astype(BF16), preferred_element_type=F32) + bs_ref[g]
            y_ref[rows, cols] = (u * vs).astype(y_ref.dtype)


def _xa_body(q_ref, k_ref, v_ref, y_ref, *, heads):
    dh = q_ref.shape[1] // heads
    for h in range(heads):
        cols = slice(h * dh, (h + 1) * dh)
        s = lax.dot_general(q_ref[:, cols], k_ref[:, cols], NT_DIMS,
                            preferred_element_type=F32) * (dh ** -0.5)
        p = jnp.exp(s - jnp.max(s, axis=-1, keepdims=True))
        p = p / jnp.sum(p, axis=-1, keepdims=True)
        y_ref[:, cols] = jnp.dot(p.astype(BF16), v_ref[:, cols],
                                 preferred_element_type=F32).astype(y_ref.dtype)


def _merge_body(su0_ref, sv0_ref, xq0_ref, mk0_ref, mv0_ref,
                su_ref, sv_ref, xq_ref, mk_ref, mv_ref,
                ya_ref, ga_ref, gb_ref, gc_ref, lg_ref, lb_ref, ws_ref, bs_ref,
                wb_ref, wo_ref, x_ref, o_ref,
                yb_cur, yc_cur, yb_nxt, yc_nxt, *, groups, chunk, heads):
    def branches(u_ref, v_ref, q_ref, k_ref, w_ref, yb_ref, yc_ref):
        _sg_body(u_ref, v_ref, lg_ref, lb_ref, ws_ref, bs_ref, yb_ref, groups=groups, chunk=chunk)
        _xa_body(q_ref, k_ref, w_ref, yc_ref, heads=heads)

    @pl.when(pl.program_id(0) == 0)
    def _():
        branches(su0_ref, sv0_ref, xq0_ref, mk0_ref, mv0_ref, yb_cur, yc_cur)

    merged = None
    for i, (y_ref, g_ref) in enumerate(((ya_ref, ga_ref), (yb_cur, gb_ref), (yc_cur, gc_ref))):
        t = jnp.dot(y_ref[...], wb_ref[i], preferred_element_type=F32)
        t = jax.nn.sigmoid(g_ref[...].astype(F32)) * t
        merged = t if merged is None else merged + t
    o_ref[...] = x_ref[...] + jnp.dot(merged.astype(BF16), wo_ref[...],
                                      preferred_element_type=F32)

    branches(su_ref, sv_ref, xq_ref, mk_ref, mv_ref, yb_nxt, yc_nxt)
    yb_cur[...] = yb_nxt[...]
    yc_cur[...] = yc_nxt[...]


def _merge(ya, proj, kv, ln_g, ln_b, w_s, b_s, wb, wo, x, *, u_block, q_block, gate_block,
           seq, tm=256):
    n, d = x.shape
    bw = ya.shape[1]
    groups, gdim = ln_g.shape
    chunk = w_s.shape[1]
    mem_len = kv.shape[0] // (n // seq)
    tiles_per_seq = seq // tm
    last = n // tm - 1
    assert tm % chunk == 0 and groups * gdim == bw and kv.shape[1] == 2 * bw
    row = lambda i: (i, 0)
    nxt = lambda i: jnp.minimum(i + 1, last)
    const2 = lambda i: (0, 0)
    const3 = lambda i: (0, 0, 0)
    once = pl.Buffered(1)
    tile = lambda f: pl.BlockSpec((tm, bw), f)
    mem = lambda f: pl.BlockSpec((mem_len, bw), f)
    return pl.pallas_call(
        functools.partial(_merge_body, groups=groups, chunk=chunk, heads=XA_HEADS),
        grid=(n // tm,),
        in_specs=[
            tile(lambda i: (0, u_block)), tile(lambda i: (0, u_block + 1)),
            tile(lambda i: (0, q_block)),
            mem(lambda i: (0, 0)), mem(lambda i: (0, 1)),
            tile(lambda i: (nxt(i), u_block)), tile(lambda i: (nxt(i), u_block + 1)),
            tile(lambda i: (nxt(i), q_block)),
            mem(lambda i: (nxt(i) // tiles_per_seq, 0)), mem(lambda i: (nxt(i) // tiles_per_seq, 1)),
            tile(row),
            pl.BlockSpec((tm, d), lambda i: (i, gate_block)),
            pl.BlockSpec((tm, d), lambda i: (i, gate_block + 1)),
            pl.BlockSpec((tm, d), lambda i: (i, gate_block + 2)),
            pl.BlockSpec((groups, gdim), const2),
            pl.BlockSpec((groups, gdim), const2),
            pl.BlockSpec((groups, chunk, chunk), const3),
            pl.BlockSpec((groups, chunk, 1), const3),
            pl.BlockSpec(wb.shape, const3, pipeline_mode=once),
            pl.BlockSpec(wo.shape, const2, pipeline_mode=once),
            pl.BlockSpec((tm, d), row),
        ],
        out_specs=pl.BlockSpec((tm, d), row),
        out_shape=jax.ShapeDtypeStruct((n, d), F32),
        scratch_shapes=[pltpu.VMEM((tm, bw), BF16)] * 4,
        compiler_params=_params(("arbitrary",)),
        name="merge",
    )(proj, proj, proj, kv, kv, proj, proj, proj, kv, kv, ya, proj, proj, proj,
      ln_g, ln_b, w_s, b_s, wb, wo, x)


def kernel(x, mem, ffn1_norm, ffn1_w_gate, ffn1_w_up, ffn1_w_down, mix_norm, mem_norm, w_in, gla_w_gate_up, gla_gate_bias, gla_out_norm, sg_ln_g, sg_ln_b, sg_w_s, sg_b_s, w_kv_mem, w_branch, w_out, ffn2_norm, ffn2_w_gate, ffn2_w_up, ffn2_w_down, final_norm):
    batch, seq, d = x.shape
    depth = w_in.shape[0]
    hk = gla_w_gate_up.shape[2]
    hv = gla_out_norm.shape[1] * gla_out_norm.shape[2]
    bw = d // 2
    rank = gla_w_gate_up.shape[1]
    lr_lo = 2 * hk + 2 * hv

    xf = x.reshape(batch * seq, d)
    memf = mem.reshape(-1, d)
    for l in range(depth):
        last = l == depth - 1
        x1, wg2, wu2, wd2 = _ffn(
            xf, ffn1_norm[l][None], ffn1_w_gate[l].astype(BF16), ffn1_w_up[l].astype(BF16),
            ffn1_w_down[l].astype(BF16),
            casts=(ffn2_w_gate[l], ffn2_w_up[l], ffn2_w_down[l]))

        w_in_t = jnp.swapaxes(w_in[l], 0, 1)
        w_main, w_lr = _regroup(w_in_t, cut_lo=lr_lo, gap=rank)
        proj, h = _proj(x1, mix_norm[l][None], w_main)
        w_up = jnp.pad(gla_w_gate_up[l], ((0, LANES - rank), (0, 0)))
        w_la = _matmul(w_lr, w_up, BF16, tm=d, tn=hk, name="fold_gate", precision=HIGHEST)
        la = _la_proj(h, w_la, gla_gate_bias[l][None])
        kv = _norm_matmul(memf, mem_norm[l][None], w_kv_mem[l].astype(BF16), BF16,
                          tm=512, name="mem_kv")

        y_gla = _gla(proj, la, gla_out_norm[l].reshape(1, hv), batch=batch, seq=seq)
        x2 = _merge(y_gla, proj, kv, sg_ln_g[l], sg_ln_b[l], sg_w_s[l], sg_b_s[l][:, :, None],
                    w_branch[l].astype(BF16), w_out[l].astype(BF16), x1,
                    u_block=lr_lo // bw, q_block=lr_lo // bw + 2,
                    gate_block=(lr_lo + 3 * bw) // d, seq=seq)

        (xf,) = _ffn(x2, ffn2_norm[l][None], wg2, wu2, wd2,
                     post_g=final_norm[None] if last else None)
    return xf.reshape(batch, seq, d)
```

```python
import functools

import jax
import jax.numpy as jnp
from jax import lax
from jax.experimental import pallas as pl
from jax.experimental.pallas import tpu as pltpu

F32 = jnp.float32
BF16 = jnp.bfloat16
HIGHEST = lax.Precision.HIGHEST
NT_DIMS = (((1,), (1,)), ((), ()))
TN_DIMS = (((0,), (0,)), ((), ()))

EPS = 1e-6
LOG2_E = 1.4426950408889634
GLA_HEADS = 4
GLA_TAU = 16.0
GLA_CHUNK = 128
GLA_SUB = 8
XA_HEADS = 4

ROW_SLAB = 256
LANES = 128
VMEM_LIMIT = 60 * 1024 * 1024


def _params(semantics):
    return pltpu.CompilerParams(dimension_semantics=semantics,
                                vmem_limit_bytes=VMEM_LIMIT)


def _rms(x, g):
    return x * lax.rsqrt(jnp.mean(x * x, axis=-1, keepdims=True) + EPS) * g


def _ffn_body(x_ref, g_ref, wg_ref, wu_ref, wd_ref, *refs, post_norm, keep_resid, n_casts,
              half):
    refs = list(refs)
    pg_ref = refs.pop(0) if post_norm else None
    cast_in = [refs.pop(0) for _ in range(n_casts)]
    o_ref = refs.pop(0)
    emit_h = post_norm and keep_resid
    h_hbm = refs.pop(0) if emit_h else None
    cast_out = [refs.pop(0) for _ in range(n_casts)]
    xn_ref = refs.pop(0)
    if emit_h:
        h_buf, h_sem = refs
    i = pl.program_id(0)
    f = pl.program_id(1)
    tm = x_ref.shape[0]

    def h_copy(tile):
        return pltpu.make_async_copy(h_buf, h_hbm.at[pl.ds(tile * tm, tm), :], h_sem)

    slabs = [slice(r, r + ROW_SLAB) for r in range(0, tm, ROW_SLAB)]

    @pl.when(f == 0)
    def _():
        for rows in slabs:
            xn_ref[rows, :] = _rms(x_ref[rows, :], g_ref[...]).astype(BF16)
        o_ref[...] = jnp.zeros_like(o_ref)

    for src_ref, dst_ref in zip(cast_in, cast_out):
        dst_ref[...] = src_ref[...].astype(BF16)

    for r in range(0, x_ref.shape[0], half):
        rows = slice(r, r + half)
        xn = xn_ref[rows, :]
        gate = jnp.dot(xn, wg_ref[...], preferred_element_type=F32)
        up = jnp.dot(xn, wu_ref[...], preferred_element_type=F32)
        hid = (jax.nn.silu(gate) * up).astype(BF16)
        o_ref[rows, :] += jnp.dot(hid, wd_ref[...], preferred_element_type=F32)

    @pl.when(f == pl.num_programs(1) - 1)
    def _():
        if emit_h:
            @pl.when(i > 0)
            def _():
                h_copy(i - 1).wait()

        for rows in slabs:
            y = x_ref[rows, :] + 0.5 * o_ref[rows, :]
            o_ref[rows, :] = y if keep_resid else _rms(y, pg_ref[...])
            if emit_h:
                h_buf[rows, :] = _rms(y, pg_ref[...]).astype(BF16)

        if emit_h:
            h_copy(i).start()

            @pl.when(i == pl.num_programs(0) - 1)
            def _():
                h_copy(i).wait()


def _ffn(x, g, wg, wu, wd, post_g=None, casts=(), *, keep_resid, tm=1024, tf=512, half=512):
    assert keep_resid or post_g is not None
    emit_h = keep_resid and post_g is not None
    n, d = x.shape
    dff = wg.shape[1]
    ni, nf = n // tm, dff // tf
    row = lambda i, f: (i, 0)
    vec = pl.BlockSpec((1, d), lambda i, f: (0, 0))
    in_specs = [pl.BlockSpec((tm, d), row), vec,
                pl.BlockSpec((d, tf), lambda i, f: (0, f)),
                pl.BlockSpec((d, tf), lambda i, f: (0, f)),
                pl.BlockSpec((tf, d), lambda i, f: (f, 0))]
    args = [x, g, wg, wu, wd]
    if post_g is not None:
        in_specs.append(vec)
        args.append(post_g)
    cast_specs = []
    for a in casts:
        rws, cls = a.shape
        if cls % nf == 0 and (cls // nf) % LANES == 0 and rws % ni == 0:
            cast_specs.append(pl.BlockSpec((rws // ni, cls // nf), lambda i, f: (i, f)))
        else:
            assert rws % (ni * nf) == 0
            cast_specs.append(pl.BlockSpec((rws // (ni * nf), cls),
                                           lambda i, f: (i * nf + f, 0)))
    out_specs = [pl.BlockSpec((tm, d), row)]
    out_shape = [jax.ShapeDtypeStruct((n, d), F32)]
    scratch = [pltpu.VMEM((tm, d), BF16)]
    if emit_h:
        out_specs.append(pl.BlockSpec(memory_space=pl.ANY))
        out_shape.append(jax.ShapeDtypeStruct((n, d), BF16))
        scratch += [pltpu.VMEM((tm, d), BF16), pltpu.SemaphoreType.DMA(())]
    return pl.pallas_call(
        functools.partial(_ffn_body, post_norm=post_g is not None, keep_resid=keep_resid,
                          n_casts=len(casts), half=half),
        grid=(ni, nf),
        in_specs=in_specs + cast_specs,
        out_specs=out_specs + cast_specs,
        out_shape=out_shape + [jax.ShapeDtypeStruct(a.shape, BF16) for a in casts],
        scratch_shapes=scratch,
        compiler_params=_params(("arbitrary", "arbitrary")),
        name="ffn",
    )(*args, *casts)


def _regroup_body(a_ref, b_ref, o_ref, cut_ref, *, cut_blocks, gap):
    c = pl.program_id(0)

    @pl.when(c < cut_blocks)
    def _():
        o_ref[...] = a_ref[...].T.astype(BF16)

    @pl.when(c == cut_blocks - 1)
    def _():
        pad = jnp.zeros((cut_ref.shape[1] - gap, b_ref.shape[1]), F32)
        cut_ref[...] = jnp.concatenate([b_ref[...], pad], axis=0).T

    @pl.when(c >= cut_blocks)
    def _():
        rows = jnp.concatenate([a_ref[gap:, :], b_ref[...]], axis=0)
        o_ref[...] = rows.T.astype(BF16)


def _regroup(wt, *, cut_lo, gap, tc=512):
    width, d = wt.shape
    out_w = width - gap
    assert cut_lo % tc == 0 and cut_lo >= tc and out_w % tc == 0 and tc % gap == 0
    return pl.pallas_call(
        functools.partial(_regroup_body, cut_blocks=cut_lo // tc, gap=gap),
        grid=(out_w // tc,),
        in_specs=[pl.BlockSpec((tc, d), lambda c: (c, 0)),
                  pl.BlockSpec((gap, d), lambda c: ((c + 1) * (tc // gap), 0))],
        out_specs=[pl.BlockSpec((d, tc), lambda c: (0, c)),
                   pl.BlockSpec((d, LANES), lambda c: (0, 0))],
        out_shape=[jax.ShapeDtypeStruct((d, out_w), BF16),
                   jax.ShapeDtypeStruct((d, LANES), F32)],
        compiler_params=_params(("arbitrary",)),
        name="regroup",
    )(wt, wt)


def _matmul_body(x_ref, w_ref, o_ref, *, precision):
    o_ref[...] = jnp.dot(x_ref[...], w_ref[...], preferred_element_type=F32,
                         precision=precision).astype(o_ref.dtype)


def _matmul(x, w, out_dtype, *, tm, tn, name, precision=None):
    n, d = x.shape
    nc = w.shape[1]
    return pl.pallas_call(
        functools.partial(_matmul_body, precision=precision),
        grid=(n // tm, nc // tn),
        in_specs=[pl.BlockSpec((tm, d), lambda i, j: (i, 0)),
                  pl.BlockSpec((d, tn), lambda i, j: (0, j))],
        out_specs=pl.BlockSpec((tm, tn), lambda i, j: (i, j)),
        out_shape=jax.ShapeDtypeStruct((n, nc), out_dtype),
        compiler_params=_params(("parallel", "parallel")),
        name=name,
    )(x, w)


def _proj_body(x_ref, w_ref, *refs, n_casts):
    cast_in, o_ref, cast_out = refs[:n_casts], refs[n_casts], refs[n_casts + 1:]
    o_ref[...] = jnp.dot(x_ref[...], w_ref[...],
                         preferred_element_type=F32).astype(o_ref.dtype)

    @pl.when(pl.program_id(1) == 0)
    def _():
        for src_ref, dst_ref in zip(cast_in, cast_out):
            dst_ref[...] = src_ref[...].astype(BF16)


def _proj(x, w, casts=(), *, tm=1024, tn=1024):
    n, d = x.shape
    nc = w.shape[1]
    ni = n // tm
    cast_specs = [pl.BlockSpec((a.shape[0] // ni, a.shape[1]), lambda i, j: (i, 0))
                  for a in casts]
    assert all(a.shape[0] % ni == 0 for a in casts)
    return pl.pallas_call(
        functools.partial(_proj_body, n_casts=len(casts)),
        grid=(ni, nc // tn),
        in_specs=[pl.BlockSpec((tm, d), lambda i, j: (i, 0)),
                  pl.BlockSpec((d, tn), lambda i, j: (0, j))] + cast_specs,
        out_specs=[pl.BlockSpec((tm, tn), lambda i, j: (i, j))] + cast_specs,
        out_shape=[jax.ShapeDtypeStruct((n, nc), BF16)]
                  + [jax.ShapeDtypeStruct(a.shape, BF16) for a in casts],
        compiler_params=_params(("parallel", "arbitrary")),
        name="proj",
    )(x, w, *casts)


def _norm_matmul_body(x_ref, g_ref, w_ref, o_ref):
    xn = _rms(x_ref[...], g_ref[...]).astype(BF16)
    o_ref[...] = jnp.dot(xn, w_ref[...], preferred_element_type=F32).astype(o_ref.dtype)


def _norm_matmul(x, g, w, out_dtype, *, tm, name):
    n, d = x.shape
    nc = w.shape[1]
    return pl.pallas_call(
        _norm_matmul_body,
        grid=(n // tm,),
        in_specs=[pl.BlockSpec((tm, d), lambda i: (i, 0)),
                  pl.BlockSpec((1, d), lambda i: (0, 0)),
                  pl.BlockSpec((d, nc), lambda i: (0, 0))],
        out_specs=pl.BlockSpec((tm, nc), lambda i: (i, 0)),
        out_shape=jax.ShapeDtypeStruct((n, nc), out_dtype),
        compiler_params=_params(("parallel",)),
        name=name,
    )(x, g, w)


def _la_body(x_ref, w_ref, b_ref, o_ref):
    z = jnp.dot(x_ref[...], w_ref[...], preferred_element_type=F32) + b_ref[...]
    o_ref[...] = jax.nn.log_sigmoid(z) * (1.0 / GLA_TAU)


def _la_proj(x, w, bias, *, tm=1024):
    n, d = x.shape
    nc = w.shape[1]
    return pl.pallas_call(
        _la_body,
        grid=(n // tm,),
        in_specs=[pl.BlockSpec((tm, d), lambda i: (i, 0)),
                  pl.BlockSpec((d, nc), lambda i: (0, 0)),
                  pl.BlockSpec((1, nc), lambda i: (0, 0))],
        out_specs=pl.BlockSpec((tm, nc), lambda i: (i, 0)),
        out_shape=jax.ShapeDtypeStruct((n, nc), F32),
        compiler_params=_params(("parallel",)),
        name="la_proj",
    )(x, w, bias)


def _gla_body(q_ref, k_ref, v_ref, r_ref, la_ref, on_ref, y_ref, st_ref, kb_ref,
              *, heads, chunk, sub):
    ts = q_ref.shape[0]
    dk = q_ref.shape[1] // heads
    dv = v_ref.shape[1] // heads
    scale = dk ** -0.5

    @pl.when(pl.program_id(1) == 0)
    def _():
        st_ref[...] = jnp.zeros_like(st_ref)

    t_idx = lax.broadcasted_iota(jnp.int32, (chunk, chunk), 0)
    s_idx = lax.broadcasted_iota(jnp.int32, (chunk, chunk), 1)
    causal = t_idx >= s_idx
    ones_tril = causal.astype(BF16)
    lane_in_sub = lax.broadcasted_iota(jnp.int32, (sub, chunk), 1) % sub

    halves = []
    m = chunk // 2
    while m >= sub:
        halves.append(m)
        m //= 2
    lane_half = {m: lax.broadcasted_iota(jnp.int32, (m, chunk), 1) >> (m.bit_length() - 1)
                 for m in halves}

    def chunk_step(c, carry):
        r0 = pl.multiple_of(c * chunk, chunk)
        rows = pl.ds(r0, chunk)
        la = la_ref[rows, :] * LOG2_E
        la_hi = la.astype(BF16)
        la_lo = (la - la_hi.astype(F32)).astype(BF16)
        b_all = (jnp.dot(ones_tril, la_hi, preferred_element_type=F32)
                 + jnp.dot(ones_tril, la_lo, preferred_element_type=F32))
        for h in range(heads):
            kcols = slice(h * dk, (h + 1) * dk)
            vcols = slice(h * dv, (h + 1) * dv)
            b = b_all[:, kcols]
            q = q_ref[rows, kcols].astype(F32) * scale
            k = k_ref[rows, kcols].astype(F32)
            v = v_ref[rows, vcols]
            st = st_ref[h]
            inter = lax.dot_general((q * jnp.exp2(b)).astype(BF16), st.astype(BF16),
                                    NT_DIMS, preferred_element_type=F32)

            kb_ref[h, 0] = k
            kb_ref[h, 1] = b
            diag = []
            for i in range(chunk // sub):
                lo = i * sub
                qi, bi = q[lo:lo + sub], b[lo:lo + sub]
                d = jnp.zeros((sub, chunk), F32)
                for s in range(sub):
                    ks = jnp.broadcast_to(kb_ref[h, 0, lo + s:lo + s + 1, :], (sub, dk))
                    bs = jnp.broadcast_to(kb_ref[h, 1, lo + s:lo + s + 1, :], (sub, dk))
                    col = jnp.sum(qi * ks * jnp.exp2(bi - bs), axis=-1, keepdims=True)
                    d = jnp.where(lane_in_sub == s, col, d)
                diag.append(d)
            scores = jnp.concatenate(diag, axis=0)

            for m in halves:
                parts = []
                for j in range(0, chunk, 2 * m):
                    b_mid = b[j + m - 1:j + m]
                    parts.append(k[j:j + m] * jnp.exp2(b_mid - b[j:j + m]))
                    parts.append(q[j + m:j + 2 * m] * jnp.exp2(b[j + m:j + 2 * m] - b_mid))
                z = jnp.concatenate(parts, axis=0).astype(BF16)
                pair = lax.dot_general(z, z, NT_DIMS, preferred_element_type=F32)
                out = []
                for j in range(0, chunk, 2 * m):
                    out.append(scores[j:j + m])
                    out.append(jnp.where(lane_half[m] == j // m, pair[j + m:j + 2 * m],
                                         scores[j + m:j + 2 * m]))
                scores = jnp.concatenate(out, axis=0)
            scores = jnp.where(causal, scores, 0.0)
            o = inter + jnp.dot(scores.astype(BF16), v, preferred_element_type=F32)

            b_end = b[chunk - 1:chunk]
            k_end = (k * jnp.exp2(b_end - b)).astype(BF16)
            upd = lax.dot_general(v, k_end, TN_DIMS, preferred_element_type=F32)
            st_ref[h] = st * jnp.exp2(b_end) + upd

            o = o * lax.rsqrt(jnp.mean(o * o, axis=-1, keepdims=True) + EPS)
            o = o * on_ref[:, vcols] * jax.nn.silu(r_ref[rows, vcols].astype(F32))
            y_ref[rows, vcols] = o.astype(y_ref.dtype)
        return carry

    lax.fori_loop(0, ts // chunk, chunk_step, 0)


def _gla(proj, la, onorm, *, batch, seq, ts=512):
    hk = la.shape[1]
    hv = onorm.shape[1]
    ns = seq // ts
    row = lambda b, s: b * ns + s
    return pl.pallas_call(
        functools.partial(_gla_body, heads=GLA_HEADS, chunk=GLA_CHUNK, sub=GLA_SUB),
        grid=(batch, ns),
        in_specs=[
            pl.BlockSpec((ts, hk), lambda b, s: (row(b, s), 0)),
            pl.BlockSpec((ts, hk), lambda b, s: (row(b, s), 1)),
            pl.BlockSpec((ts, hv), lambda b, s: (row(b, s), hk * 2 // hv)),
            pl.BlockSpec((ts, hv), lambda b, s: (row(b, s), hk * 2 // hv + 1)),
            pl.BlockSpec((ts, hk), lambda b, s: (row(b, s), 0)),
            pl.BlockSpec((1, hv), lambda b, s: (0, 0)),
        ],
        out_specs=pl.BlockSpec((ts, hv), lambda b, s: (row(b, s), 0)),
        out_shape=jax.ShapeDtypeStruct((batch * seq, hv), BF16),
        scratch_shapes=[pltpu.VMEM((GLA_HEADS, hv // GLA_HEADS, hk // GLA_HEADS), F32),
                        pltpu.VMEM((GLA_HEADS, 2, GLA_CHUNK, hk // GLA_HEADS), F32)],
        compiler_params=_params(("parallel", "arbitrary")),
        name="gla",
    )(proj, proj, proj, proj, la, onorm)


def _gelu(x):
    return 0.5 * x * (1.0 + lax.erf(x * (2.0 ** -0.5)))


def _sg_body(u_ref, v_ref, lg_ref, lb_ref, ws_ref, bs_ref, y_ref, *, groups, chunk):
    ts = u_ref.shape[0]
    gdim = u_ref.shape[1] // groups
    rr = lax.broadcasted_iota(jnp.int32, (chunk, chunk), 0)
    cc = lax.broadcasted_iota(jnp.int32, (chunk, chunk), 1)
    for g in range(groups):
        cols = slice(g * gdim, (g + 1) * gdim)
        w = jnp.where(rr >= cc, ws_ref[g], 0.0).astype(BF16)
        for c in range(ts // chunk):
            rows = slice(c * chunk, (c + 1) * chunk)
            u = _gelu(u_ref[rows, cols].astype(F32))
            v = _gelu(v_ref[rows, cols].astype(F32))
            mu = jnp.mean(v, axis=-1, keepdims=True)
            var = jnp.mean((v - mu) ** 2, axis=-1, keepdims=True)
            vn = (v - mu) * lax.rsqrt(var + EPS) * lg_ref[g:g + 1, :] + lb_ref[g:g + 1, :]
            vs = jnp.dot(w, vn.astype(BF16), preferred_element_type=F32) + bs_ref[g]
            y_ref[rows, cols] = (u * vs).astype(y_ref.dtype)


def _xa_body(q_ref, k_ref, v_ref, y_ref, *, heads):
    dh = q_ref.shape[1] // heads
    for h in range(heads):
        cols = slice(h * dh, (h + 1) * dh)
        s = lax.dot_general(q_ref[:, cols], k_ref[:, cols], NT_DIMS,
                            preferred_element_type=F32) * (dh ** -0.5)
        p = jnp.exp(s - jnp.max(s, axis=-1, keepdims=True))
        p = p / jnp.sum(p, axis=-1, keepdims=True)
        y_ref[:, cols] = jnp.dot(p.astype(BF16), v_ref[:, cols],
                                 preferred_element_type=F32).astype(y_ref.dtype)


def _merge_body(su0_ref, sv0_ref, xq0_ref, mk0_ref, mv0_ref,
                su_ref, sv_ref, xq_ref, mk_ref, mv_ref,
                ya_ref, ga_ref, gb_ref, gc_ref, lg_ref, lb_ref, ws_ref, bs_ref,
                wb_ref, wo_ref, x_ref, o_ref,
                yb_cur, yc_cur, yb_nxt, yc_nxt, *, groups, chunk, heads):
    def branches(u_ref, v_ref, q_ref, k_ref, w_ref, yb_ref, yc_ref):
        _sg_body(u_ref, v_ref, lg_ref, lb_ref, ws_ref, bs_ref, yb_ref, groups=groups, chunk=chunk)
        _xa_body(q_ref, k_ref, w_ref, yc_ref, heads=heads)

    @pl.when(pl.program_id(0) == 0)
    def _():
        branches(su0_ref, sv0_ref, xq0_ref, mk0_ref, mv0_ref, yb_cur, yc_cur)

    merged = None
    for i, (y_ref, g_ref) in enumerate(((ya_ref, ga_ref), (yb_cur, gb_ref), (yc_cur, gc_ref))):
        t = jnp.dot(y_ref[...], wb_ref[i], preferred_element_type=F32)
        t = jax.nn.sigmoid(g_ref[...].astype(F32)) * t
        merged = t if merged is None else merged + t
    o_ref[...] = x_ref[...] + jnp.dot(merged.astype(BF16), wo_ref[...],
                                      preferred_element_type=F32)

    branches(su_ref, sv_ref, xq_ref, mk_ref, mv_ref, yb_nxt, yc_nxt)
    yb_cur[...] = yb_nxt[...]
    yc_cur[...] = yc_nxt[...]


def _merge(ya, proj, kv, ln_g, ln_b, w_s, b_s, wb, wo, x, *, u_block, q_block, gate_block,
           seq, tm=256):
    n, d = x.shape
    bw = ya.shape[1]
    groups, gdim = ln_g.shape
    chunk = w_s.shape[1]
    mem_len = kv.shape[0] // (n // seq)
    tiles_per_seq = seq // tm
    last = n // tm - 1
    assert tm % chunk == 0 and groups * gdim == bw and kv.shape[1] == 2 * bw
    row = lambda i: (i, 0)
    nxt = lambda i: jnp.minimum(i + 1, last)
    const2 = lambda i: (0, 0)
    const3 = lambda i: (0, 0, 0)
    once = pl.Buffered(1)
    tile = lambda f: pl.BlockSpec((tm, bw), f)
    mem = lambda f: pl.BlockSpec((mem_len, bw), f)
    return pl.pallas_call(
        functools.partial(_merge_body, groups=groups, chunk=chunk, heads=XA_HEADS),
        grid=(n // tm,),
        in_specs=[
            tile(lambda i: (0, u_block)), tile(lambda i: (0, u_block + 1)),
            tile(lambda i: (0, q_block)),
            mem(lambda i: (0, 0)), mem(lambda i: (0, 1)),
            tile(lambda i: (nxt(i), u_block)), tile(lambda i: (nxt(i), u_block + 1)),
            tile(lambda i: (nxt(i), q_block)),
            mem(lambda i: (nxt(i) // tiles_per_seq, 0)), mem(lambda i: (nxt(i) // tiles_per_seq, 1)),
            tile(row),
            pl.BlockSpec((tm, d), lambda i: (i, gate_block)),
            pl.BlockSpec((tm, d), lambda i: (i, gate_block + 1)),
            pl.BlockSpec((tm, d), lambda i: (i, gate_block + 2)),
            pl.BlockSpec((groups, gdim), const2),
            pl.BlockSpec((groups, gdim), const2),
            pl.BlockSpec((groups, chunk, chunk), const3),
            pl.BlockSpec((groups, chunk, 1), const3),
            pl.BlockSpec(wb.shape, const3, pipeline_mode=once),
            pl.BlockSpec(wo.shape, const2, pipeline_mode=once),
            pl.BlockSpec((tm, d), row),
        ],
        out_specs=pl.BlockSpec((tm, d), row),
        out_shape=jax.ShapeDtypeStruct((n, d), F32),
        scratch_shapes=[pltpu.VMEM((tm, bw), BF16)] * 4,
        compiler_params=_params(("arbitrary",)),
        name="merge",
    )(proj, proj, proj, kv, kv, proj, proj, proj, kv, kv, ya, proj, proj, proj,
      ln_g, ln_b, w_s, b_s, wb, wo, x)


def kernel(x, mem, ffn1_norm, ffn1_w_gate, ffn1_w_up, ffn1_w_down, mix_norm, mem_norm, w_in, gla_w_gate_up, gla_gate_bias, gla_out_norm, sg_ln_g, sg_ln_b, sg_w_s, sg_b_s, w_kv_mem, w_branch, w_out, ffn2_norm, ffn2_w_gate, ffn2_w_up, ffn2_w_down, final_norm):
    batch, seq, d = x.shape
    depth = w_in.shape[0]
    hk = gla_w_gate_up.shape[2]
    hv = gla_out_norm.shape[1] * gla_out_norm.shape[2]
    bw = d // 2
    rank = gla_w_gate_up.shape[1]
    lr_lo = 2 * hk + 2 * hv

    xf = x.reshape(batch * seq, d)
    memf = mem.reshape(-1, d)
    for l in range(depth):
        last = l == depth - 1
        x1, h, wg2, wu2, wd2 = _ffn(
            xf, ffn1_norm[l][None], ffn1_w_gate[l].astype(BF16), ffn1_w_up[l].astype(BF16),
            ffn1_w_down[l].astype(BF16), post_g=mix_norm[l][None],
            casts=(ffn2_w_gate[l], ffn2_w_up[l], ffn2_w_down[l]), keep_resid=True)

        w_in_t = jnp.swapaxes(w_in[l], 0, 1)
        w_main, w_lr = _regroup(w_in_t, cut_lo=lr_lo, gap=rank)
        proj, wb, wo, wkv = _proj(h, w_main,
                                  casts=(w_branch[l].reshape(-1, d), w_out[l], w_kv_mem[l]))
        w_up = jnp.pad(gla_w_gate_up[l], ((0, LANES - rank), (0, 0)))
        w_la = _matmul(w_lr, w_up, BF16, tm=d, tn=hk, name="fold_gate", precision=HIGHEST)
        la = _la_proj(h, w_la, gla_gate_bias[l][None])
        kv = _norm_matmul(memf, mem_norm[l][None], wkv, BF16, tm=512, name="mem_kv")

        y_gla = _gla(proj, la, gla_out_norm[l].reshape(1, hv), batch=batch, seq=seq)
        x2 = _merge(y_gla, proj, kv, sg_ln_g[l], sg_ln_b[l], sg_w_s[l], sg_b_s[l][:, :, None],
                    wb.reshape(w_branch[l].shape), wo, x1,
                    u_block=lr_lo // bw, q_block=lr_lo // bw + 2,
                    gate_block=(lr_lo + 3 * bw) // d, seq=seq)

        (xf,) = _ffn(x2, ffn2_norm[l][None], wg2, wu2, wd2,
                     post_g=final_norm[None] if last else None, keep_resid=not last)
    return xf.reshape(batch, seq, d)
```

```python
import functools

import jax
import jax.numpy as jnp
from jax import lax
from jax.experimental import pallas as pl
from jax.experimental.pallas import tpu as pltpu

F32 = jnp.float32
BF16 = jnp.bfloat16
HIGHEST = lax.Precision.HIGHEST
NT_DIMS = (((1,), (1,)), ((), ()))
TN_DIMS = (((0,), (0,)), ((), ()))

EPS = 1e-6
LOG2_E = 1.4426950408889634
GLA_HEADS = 4
GLA_TAU = 16.0
GLA_CHUNK = 128
GLA_SUB = 8
XA_HEADS = 4

ROW_SLAB = 256
LANES = 128
VMEM_LIMIT = 60 * 1024 * 1024


def _params(semantics):
    return pltpu.CompilerParams(dimension_semantics=semantics,
                                vmem_limit_bytes=VMEM_LIMIT)


def _rms(x, g):
    return x * lax.rsqrt(jnp.mean(x * x, axis=-1, keepdims=True) + EPS) * g


def _ffn_body(x_ref, g_ref, wg_ref, wu_ref, wd_ref, *refs, post_norm, keep_resid, n_casts,
              half):
    refs = list(refs)
    pg_ref = refs.pop(0) if post_norm else None
    cast_in = [refs.pop(0) for _ in range(n_casts)]
    o_ref = refs.pop(0)
    emit_h = post_norm and keep_resid
    h_hbm = refs.pop(0) if emit_h else None
    cast_out = [refs.pop(0) for _ in range(n_casts)]
    xn_ref = refs.pop(0)
    if emit_h:
        h_buf, h_sem = refs
    i = pl.program_id(0)
    f = pl.program_id(1)
    tm = x_ref.shape[0]

    def h_copy(tile):
        return pltpu.make_async_copy(h_buf, h_hbm.at[pl.ds(tile * tm, tm), :], h_sem)

    slabs = [slice(r, r + ROW_SLAB) for r in range(0, tm, ROW_SLAB)]

    @pl.when(f == 0)
    def _():
        for rows in slabs:
            x = x_ref[rows, :]
            xn_ref[rows, :] = _rms(x, g_ref[...]).astype(BF16)
            o_ref[rows, :] = x

    for src_ref, dst_ref in zip(cast_in, cast_out):
        dst_ref[...] = src_ref[...].astype(BF16)

    for r in range(0, x_ref.shape[0], half):
        rows = slice(r, r + half)
        xn = xn_ref[rows, :]
        gate = jnp.dot(xn, wg_ref[...], preferred_element_type=F32)
        up = jnp.dot(xn, wu_ref[...], preferred_element_type=F32)
        hid = (jax.nn.silu(gate) * up * 0.5).astype(BF16)
        o_ref[rows, :] += jnp.dot(hid, wd_ref[...], preferred_element_type=F32)

    @pl.when(f == pl.num_programs(1) - 1)
    def _():
        if emit_h:
            @pl.when(i > 0)
            def _():
                h_copy(i - 1).wait()

        if post_norm:
            for rows in slabs:
                yn = _rms(o_ref[rows, :], pg_ref[...])
                if emit_h:
                    h_buf[rows, :] = yn.astype(BF16)
                else:
                    o_ref[rows, :] = yn

        if emit_h:
            h_copy(i).start()

            @pl.when(i == pl.num_programs(0) - 1)
            def _():
                h_copy(i).wait()


def _ffn(x, g, wg, wu, wd, post_g=None, casts=(), *, keep_resid, tm=1024, tf=512, half=512):
    assert keep_resid or post_g is not None
    emit_h = keep_resid and post_g is not None
    n, d = x.shape
    dff = wg.shape[1]
    ni, nf = n // tm, dff // tf
    row = lambda i, f: (i, 0)
    vec = pl.BlockSpec((1, d), lambda i, f: (0, 0))
    in_specs = [pl.BlockSpec((tm, d), row), vec,
                pl.BlockSpec((d, tf), lambda i, f: (0, f)),
                pl.BlockSpec((d, tf), lambda i, f: (0, f)),
                pl.BlockSpec((tf, d), lambda i, f: (f, 0))]
    args = [x, g, wg, wu, wd]
    if post_g is not None:
        in_specs.append(vec)
        args.append(post_g)
    cast_specs = []
    for a in casts:
        rws, cls = a.shape
        if cls % nf == 0 and (cls // nf) % LANES == 0 and rws % ni == 0:
            cast_specs.append(pl.BlockSpec((rws // ni, cls // nf), lambda i, f: (i, f)))
        else:
            assert rws % (ni * nf) == 0
            cast_specs.append(pl.BlockSpec((rws // (ni * nf), cls),
                                           lambda i, f: (i * nf + f, 0)))
    out_specs = [pl.BlockSpec((tm, d), row)]
    out_shape = [jax.ShapeDtypeStruct((n, d), F32)]
    scratch = [pltpu.VMEM((tm, d), BF16)]
    if emit_h:
        out_specs.append(pl.BlockSpec(memory_space=pl.ANY))
        out_shape.append(jax.ShapeDtypeStruct((n, d), BF16))
        scratch += [pltpu.VMEM((tm, d), BF16), pltpu.SemaphoreType.DMA(())]
    return pl.pallas_call(
        functools.partial(_ffn_body, post_norm=post_g is not None, keep_resid=keep_resid,
                          n_casts=len(casts), half=half),
        grid=(ni, nf),
        in_specs=in_specs + cast_specs,
        out_specs=out_specs + cast_specs,
        out_shape=out_shape + [jax.ShapeDtypeStruct(a.shape, BF16) for a in casts],
        scratch_shapes=scratch,
        compiler_params=_params(("arbitrary", "arbitrary")),
        name="ffn",
    )(*args, *casts)


def _regroup_body(a_ref, b_ref, o_ref, cut_ref, *, cut_blocks, gap):
    c = pl.program_id(0)

    @pl.when(c < cut_blocks)
    def _():
        o_ref[...] = a_ref[...].T.astype(BF16)

    @pl.when(c == cut_blocks - 1)
    def _():
        pad = jnp.zeros((cut_ref.shape[1] - gap, b_ref.shape[1]), F32)
        cut_ref[...] = jnp.concatenate([b_ref[...], pad], axis=0).T

    @pl.when(c >= cut_blocks)
    def _():
        rows = jnp.concatenate([a_ref[gap:, :], b_ref[...]], axis=0)
        o_ref[...] = rows.T.astype(BF16)


def _regroup(wt, *, cut_lo, gap, tc=512):
    width, d = wt.shape
    out_w = width - gap
    assert cut_lo % tc == 0 and cut_lo >= tc and out_w % tc == 0 and tc % gap == 0
    return pl.pallas_call(
        functools.partial(_regroup_body, cut_blocks=cut_lo // tc, gap=gap),
        grid=(out_w // tc,),
        in_specs=[pl.BlockSpec((tc, d), lambda c: (c, 0)),
                  pl.BlockSpec((gap, d), lambda c: ((c + 1) * (tc // gap), 0))],
        out_specs=[pl.BlockSpec((d, tc), lambda c: (0, c)),
                   pl.BlockSpec((d, LANES), lambda c: (0, 0))],
        out_shape=[jax.ShapeDtypeStruct((d, out_w), BF16),
                   jax.ShapeDtypeStruct((d, LANES), F32)],
        compiler_params=_params(("arbitrary",)),
        name="regroup",
    )(wt, wt)


def _matmul_body(x_ref, w_ref, o_ref, *, precision):
    o_ref[...] = jnp.dot(x_ref[...], w_ref[...], preferred_element_type=F32,
                         precision=precision).astype(o_ref.dtype)


def _matmul(x, w, out_dtype, *, tm, tn, name, precision=None):
    n, d = x.shape
    nc = w.shape[1]
    return pl.pallas_call(
        functools.partial(_matmul_body, precision=precision),
        grid=(n // tm, nc // tn),
        in_specs=[pl.BlockSpec((tm, d), lambda i, j: (i, 0)),
                  pl.BlockSpec((d, tn), lambda i, j: (0, j))],
        out_specs=pl.BlockSpec((tm, tn), lambda i, j: (i, j)),
        out_shape=jax.ShapeDtypeStruct((n, nc), out_dtype),
        compiler_params=_params(("parallel", "parallel")),
        name=name,
    )(x, w)


def _norm_matmul_body(x_ref, g_ref, w_ref, o_ref):
    xn = _rms(x_ref[...], g_ref[...]).astype(BF16)
    o_ref[...] = jnp.dot(xn, w_ref[...], preferred_element_type=F32).astype(o_ref.dtype)


def _norm_matmul(x, g, w, out_dtype, *, tm, name):
    n, d = x.shape
    nc = w.shape[1]
    return pl.pallas_call(
        _norm_matmul_body,
        grid=(n // tm,),
        in_specs=[pl.BlockSpec((tm, d), lambda i: (i, 0)),
                  pl.BlockSpec((1, d), lambda i: (0, 0)),
                  pl.BlockSpec((d, nc), lambda i: (0, 0))],
        out_specs=pl.BlockSpec((tm, nc), lambda i: (i, 0)),
        out_shape=jax.ShapeDtypeStruct((n, nc), out_dtype),
        compiler_params=_params(("parallel",)),
        name=name,
    )(x, g, w)


def _la_body(x_ref, w_ref, b_ref, *refs, n_casts):
    cast_in, o_ref, cast_out = refs[:n_casts], refs[n_casts], refs[n_casts + 1:]
    z = jnp.dot(x_ref[...], w_ref[...], preferred_element_type=F32) + b_ref[...]
    o_ref[...] = jax.nn.log_sigmoid(z) * (1.0 / GLA_TAU)
    for src_ref, dst_ref in zip(cast_in, cast_out):
        dst_ref[...] = src_ref[...].astype(BF16)


def _la_proj(x, w, bias, casts=(), *, tm=1024):
    n, d = x.shape
    nc = w.shape[1]
    ni = n // tm
    assert all(a.shape[0] % ni == 0 for a in casts)
    cast_specs = [pl.BlockSpec((a.shape[0] // ni, a.shape[1]), lambda i: (i, 0))
                  for a in casts]
    return pl.pallas_call(
        functools.partial(_la_body, n_casts=len(casts)),
        grid=(ni,),
        in_specs=[pl.BlockSpec((tm, d), lambda i: (i, 0)),
                  pl.BlockSpec((d, nc), lambda i: (0, 0)),
                  pl.BlockSpec((1, nc), lambda i: (0, 0))] + cast_specs,
        out_specs=[pl.BlockSpec((tm, nc), lambda i: (i, 0))] + cast_specs,
        out_shape=[jax.ShapeDtypeStruct((n, nc), F32)]
                  + [jax.ShapeDtypeStruct(a.shape, BF16) for a in casts],
        compiler_params=_params(("parallel",)),
        name="la_proj",
    )(x, w, bias, *casts)


def _gla_body(q_ref, k_ref, v_ref, r_ref, la_ref, on_ref, y_ref, st_ref, kb_ref,
              *, heads, chunk, sub):
    ts = q_ref.shape[0]
    dk = q_ref.shape[1] // heads
    dv = v_ref.shape[1] // heads
    scale = dk ** -0.5

    @pl.when(pl.program_id(1) == 0)
    def _():
        st_ref[...] = jnp.zeros_like(st_ref)

    t_idx = lax.broadcasted_iota(jnp.int32, (chunk, chunk), 0)
    s_idx = lax.broadcasted_iota(jnp.int32, (chunk, chunk), 1)
    causal = t_idx >= s_idx
    ones_tril = causal.astype(BF16)
    lane_in_sub = lax.broadcasted_iota(jnp.int32, (sub, chunk), 1) % sub

    halves = []
    m = chunk // 2
    while m >= sub:
        halves.append(m)
        m //= 2
    lane_half = {m: lax.broadcasted_iota(jnp.int32, (m, chunk), 1) >> (m.bit_length() - 1)
                 for m in halves}

    def chunk_step(c, carry):
        r0 = pl.multiple_of(c * chunk, chunk)
        rows = pl.ds(r0, chunk)
        la = la_ref[rows, :] * LOG2_E
        la_hi = la.astype(BF16)
        la_lo = (la - la_hi.astype(F32)).astype(BF16)
        b_all = (jnp.dot(ones_tril, la_hi, preferred_element_type=F32)
                 + jnp.dot(ones_tril, la_lo, preferred_element_type=F32))
        for h in range(heads):
            kcols = slice(h * dk, (h + 1) * dk)
            vcols = slice(h * dv, (h + 1) * dv)
            b = b_all[:, kcols]
            q = q_ref[rows, kcols].astype(F32) * scale
            k = k_ref[rows, kcols].astype(F32)
            v = v_ref[rows, vcols]
            st = st_ref[h]
            inter = lax.dot_general((q * jnp.exp2(b)).astype(BF16), st.astype(BF16),
                                    NT_DIMS, preferred_element_type=F32)

            kb_ref[h, 0] = k
            kb_ref[h, 1] = b
            diag = []
            for i in range(chunk // sub):
                lo = i * sub
                qi, bi = q[lo:lo + sub], b[lo:lo + sub]
                d = jnp.zeros((sub, chunk), F32)
                for s in range(sub):
                    ks = jnp.broadcast_to(kb_ref[h, 0, lo + s:lo + s + 1, :], (sub, dk))
                    bs = jnp.broadcast_to(kb_ref[h, 1, lo + s:lo + s + 1, :], (sub, dk))
                    col = jnp.sum(qi * ks * jnp.exp2(bi - bs), axis=-1, keepdims=True)
                    d = jnp.where(lane_in_sub == s, col, d)
                diag.append(d)
            scores = jnp.concatenate(diag, axis=0)

            for m in halves:
                parts = []
                for j in range(0, chunk, 2 * m):
                    b_mid = b[j + m - 1:j + m]
                    parts.append(k[j:j + m] * jnp.exp2(b_mid - b[j:j + m]))
                    parts.append(q[j + m:j + 2 * m] * jnp.exp2(b[j + m:j + 2 * m] - b_mid))
                z = jnp.concatenate(parts, axis=0).astype(BF16)
                pair = lax.dot_general(z, z, NT_DIMS, preferred_element_type=F32)
                out = []
                for j in range(0, chunk, 2 * m):
                    out.append(scores[j:j + m])
                    out.append(jnp.where(lane_half[m] == j // m, pair[j + m:j + 2 * m],
                                         scores[j + m:j + 2 * m]))
                scores = jnp.concatenate(out, axis=0)
            scores = jnp.where(causal, scores, 0.0)
            o = inter + jnp.dot(scores.astype(BF16), v, preferred_element_type=F32)

            b_end = b[chunk - 1:chunk]
            k_end = (k * jnp.exp2(b_end - b)).astype(BF16)
            upd = lax.dot_general(v, k_end, TN_DIMS, preferred_element_type=F32)
            st_ref[h] = st * jnp.exp2(b_end) + upd

            o = o * lax.rsqrt(jnp.mean(o * o, axis=-1, keepdims=True) + EPS)
            o = o * on_ref[:, vcols] * jax.nn.silu(r_ref[rows, vcols].astype(F32))
            y_ref[rows, vcols] = o.astype(y_ref.dtype)
        return carry

    lax.fori_loop(0, ts // chunk, chunk_step, 0)


def _gla(proj, la, onorm, *, batch, seq, ts=512):
    hk = la.shape[1]
    hv = onorm.shape[1]
    ns = seq // ts
    row = lambda b, s: b * ns + s
    return pl.pallas_call(
        functools.partial(_gla_body, heads=GLA_HEADS, chunk=GLA_CHUNK, sub=GLA_SUB),
        grid=(batch, ns),
        in_specs=[
            pl.BlockSpec((ts, hk), lambda b, s: (row(b, s), 0)),
            pl.BlockSpec((ts, hk), lambda b, s: (row(b, s), 1)),
            pl.BlockSpec((ts, hv), lambda b, s: (row(b, s), hk * 2 // hv)),
            pl.BlockSpec((ts, hv), lambda b, s: (row(b, s), hk * 2 // hv + 1)),
            pl.BlockSpec((ts, hk), lambda b, s: (row(b, s), 0)),
            pl.BlockSpec((1, hv), lambda b, s: (0, 0)),
        ],
        out_specs=pl.BlockSpec((ts, hv), lambda b, s: (row(b, s), 0)),
        out_shape=jax.ShapeDtypeStruct((batch * seq, hv), BF16),
        scratch_shapes=[pltpu.VMEM((GLA_HEADS, hv // GLA_HEADS, hk // GLA_HEADS), F32),
                        pltpu.VMEM((GLA_HEADS, 2, GLA_CHUNK, hk // GLA_HEADS), F32)],
        compiler_params=_params(("parallel", "arbitrary")),
        name="gla",
    )(proj, proj, proj, proj, la, onorm)


def _gelu(x):
    return 0.5 * x * (1.0 + lax.erf(x * (2.0 ** -0.5)))


def _sg_body(u_ref, v_ref, lg_ref, lb_ref, ws_ref, bs_ref, y_ref, *, groups, chunk):
    ts = u_ref.shape[0]
    gdim = u_ref.shape[1] // groups
    rr = lax.broadcasted_iota(jnp.int32, (chunk, chunk), 0)
    cc = lax.broadcasted_iota(jnp.int32, (chunk, chunk), 1)
    for g in range(groups):
        cols = slice(g * gdim, (g + 1) * gdim)
        w = jnp.where(rr >= cc, ws_ref[g], 0.0).astype(BF16)
        for c in range(ts // chunk):
            rows = slice(c * chunk, (c + 1) * chunk)
            u = _gelu(u_ref[rows, cols].astype(F32))
            v = _gelu(v_ref[rows, cols].astype(F32))
            mu = jnp.mean(v, axis=-1, keepdims=True)
            var = jnp.mean((v - mu) ** 2, axis=-1, keepdims=True)
            vn = (v - mu) * lax.rsqrt(var + EPS) * lg_ref[g:g + 1, :] + lb_ref[g:g + 1, :]
            vs = jnp.dot(w, vn.astype(BF16), preferred_element_type=F32) + bs_ref[g]
            y_ref[rows, cols] = (u * vs).astype(y_ref.dtype)


def _xa_body(q_ref, k_ref, v_ref, y_ref, *, heads):
    dh = q_ref.shape[1] // heads
    for h in range(heads):
        cols = slice(h * dh, (h + 1) * dh)
        s = lax.dot_general(q_ref[:, cols], k_ref[:, cols], NT_DIMS,
                            preferred_element_type=F32) * (dh ** -0.5)
        p = jnp.exp(s - jnp.max(s, axis=-1, keepdims=True))
        p = p / jnp.sum(p, axis=-1, keepdims=True)
        y_ref[:, cols] = jnp.dot(p.astype(BF16), v_ref[:, cols],
                                 preferred_element_type=F32).astype(y_ref.dtype)


def _merge_body(su0_ref, sv0_ref, xq0_ref, mk0_ref, mv0_ref,
                su_ref, sv_ref, xq_ref, mk_ref, mv_ref,
                ya_ref, ga_ref, gb_ref, gc_ref, lg_ref, lb_ref, ws_ref, bs_ref,
                wb_ref, wo_ref, x_ref, o_ref,
                yb_cur, yc_cur, yb_nxt, yc_nxt, *, groups, chunk, heads):
    def branches(u_ref, v_ref, q_ref, k_ref, w_ref, yb_ref, yc_ref):
        _sg_body(u_ref, v_ref, lg_ref, lb_ref, ws_ref, bs_ref, yb_ref, groups=groups, chunk=chunk)
        _xa_body(q_ref, k_ref, w_ref, yc_ref, heads=heads)

    @pl.when(pl.program_id(0) == 0)
    def _():
        branches(su0_ref, sv0_ref, xq0_ref, mk0_ref, mv0_ref, yb_cur, yc_cur)

    merged = None
    for i, (y_ref, g_ref) in enumerate(((ya_ref, ga_ref), (yb_cur, gb_ref), (yc_cur, gc_ref))):
        t = jnp.dot(y_ref[...], wb_ref[i], preferred_element_type=F32)
        t = jax.nn.sigmoid(g_ref[...].astype(F32)) * t
        merged = t if merged is None else merged + t
    o_ref[...] = x_ref[...] + jnp.dot(merged.astype(BF16), wo_ref[...],
                                      preferred_element_type=F32)

    branches(su_ref, sv_ref, xq_ref, mk_ref, mv_ref, yb_nxt, yc_nxt)
    yb_cur[...] = yb_nxt[...]
    yc_cur[...] = yc_nxt[...]


def _merge(ya, proj, kv, ln_g, ln_b, w_s, b_s, wb, wo, x, *, u_block, q_block, gate_block,
           seq, tm=256):
    n, d = x.shape
    bw = ya.shape[1]
    groups, gdim = ln_g.shape
    chunk = w_s.shape[1]
    mem_len = kv.shape[0] // (n // seq)
    tiles_per_seq = seq // tm
    last = n // tm - 1
    assert tm % chunk == 0 and groups * gdim == bw and kv.shape[1] == 2 * bw
    row = lambda i: (i, 0)
    nxt = lambda i: jnp.minimum(i + 1, last)
    const2 = lambda i: (0, 0)
    const3 = lambda i: (0, 0, 0)
    once = pl.Buffered(1)
    tile = lambda f: pl.BlockSpec((tm, bw), f)
    mem = lambda f: pl.BlockSpec((mem_len, bw), f)
    return pl.pallas_call(
        functools.partial(_merge_body, groups=groups, chunk=chunk, heads=XA_HEADS),
        grid=(n // tm,),
        in_specs=[
            tile(lambda i: (0, u_block)), tile(lambda i: (0, u_block + 1)),
            tile(lambda i: (0, q_block)),
            mem(lambda i: (0, 0)), mem(lambda i: (0, 1)),
            tile(lambda i: (nxt(i), u_block)), tile(lambda i: (nxt(i), u_block + 1)),
            tile(lambda i: (nxt(i), q_block)),
            mem(lambda i: (nxt(i) // tiles_per_seq, 0)), mem(lambda i: (nxt(i) // tiles_per_seq, 1)),
            tile(row),
            pl.BlockSpec((tm, d), lambda i: (i, gate_block)),
            pl.BlockSpec((tm, d), lambda i: (i, gate_block + 1)),
            pl.BlockSpec((tm, d), lambda i: (i, gate_block + 2)),
            pl.BlockSpec((groups, gdim), const2),
            pl.BlockSpec((groups, gdim), const2),
            pl.BlockSpec((groups, chunk, chunk), const3),
            pl.BlockSpec((groups, chunk, 1), const3),
            pl.BlockSpec(wb.shape, const3, pipeline_mode=once),
            pl.BlockSpec(wo.shape, const2, pipeline_mode=once),
            pl.BlockSpec((tm, d), row),
        ],
        out_specs=pl.BlockSpec((tm, d), row),
        out_shape=jax.ShapeDtypeStruct((n, d), F32),
        scratch_shapes=[pltpu.VMEM((tm, bw), BF16)] * 4,
        compiler_params=_params(("arbitrary",)),
        name="merge",
    )(proj, proj, proj, kv, kv, proj, proj, proj, kv, kv, ya, proj, proj, proj,
      ln_g, ln_b, w_s, b_s, wb, wo, x)


def kernel(x, mem, ffn1_norm, ffn1_w_gate, ffn1_w_up, ffn1_w_down, mix_norm, mem_norm, w_in, gla_w_gate_up, gla_gate_bias, gla_out_norm, sg_ln_g, sg_ln_b, sg_w_s, sg_b_s, w_kv_mem, w_branch, w_out, ffn2_norm, ffn2_w_gate, ffn2_w_up, ffn2_w_down, final_norm):
    batch, seq, d = x.shape
    depth = w_in.shape[0]
    hk = gla_w_gate_up.shape[2]
    hv = gla_out_norm.shape[1] * gla_out_norm.shape[2]
    bw = d // 2
    rank = gla_w_gate_up.shape[1]
    lr_lo = 2 * hk + 2 * hv

    xf = x.reshape(batch * seq, d)
    memf = mem.reshape(-1, d)
    for l in range(depth):
        last = l == depth - 1
        x1, h, wg2, wu2, wd2 = _ffn(
            xf, ffn1_norm[l][None], ffn1_w_gate[l].astype(BF16), ffn1_w_up[l].astype(BF16),
            ffn1_w_down[l].astype(BF16), post_g=mix_norm[l][None],
            casts=(ffn2_w_gate[l], ffn2_w_up[l], ffn2_w_down[l]), keep_resid=True)

        w_in_t = jnp.swapaxes(w_in[l], 0, 1)
        w_main, w_lr = _regroup(w_in_t, cut_lo=lr_lo, gap=rank)
        proj = _matmul(h, w_main, BF16, tm=1024, tn=1024, name="proj")
        w_up = jnp.pad(gla_w_gate_up[l], ((0, LANES - rank), (0, 0)))
        w_la = _matmul(w_lr, w_up, BF16, tm=d, tn=hk, name="fold_gate", precision=HIGHEST)
        la, wb, wo, wkv = _la_proj(h, w_la, gla_gate_bias[l][None],
                                   casts=(w_branch[l].reshape(-1, d), w_out[l], w_kv_mem[l]))
        kv = _norm_matmul(memf, mem_norm[l][None], wkv, BF16, tm=512, name="mem_kv")

        y_gla = _gla(proj, la, gla_out_norm[l].reshape(1, hv), batch=batch, seq=seq)
        x2 = _merge(y_gla, proj, kv, sg_ln_g[l], sg_ln_b[l], sg_w_s[l], sg_b_s[l][:, :, None],
                    wb.reshape(w_branch[l].shape), wo, x1,
                    u_block=lr_lo // bw, q_block=lr_lo // bw + 2,
                    gate_block=(lr_lo + 3 * bw) // d, seq=seq)

        (xf,) = _ffn(x2, ffn2_norm[l][None], wg2, wu2, wd2,
                     post_g=final_norm[None] if last else None, keep_resid=not last)
    return xf.reshape(batch, seq, d)
```

```python
import functools

import jax
import jax.numpy as jnp
from jax import lax
from jax.experimental import pallas as pl
from jax.experimental.pallas import tpu as pltpu

F32 = jnp.float32
BF16 = jnp.bfloat16
HIGHEST = lax.Precision.HIGHEST
NT_DIMS = (((1,), (1,)), ((), ()))
TN_DIMS = (((0,), (0,)), ((), ()))

EPS = 1e-6
LOG2_E = 1.4426950408889634
GLA_HEADS = 4
GLA_TAU = 16.0
GLA_CHUNK = 128
GLA_SUB = 8
XA_HEADS = 4

ROW_SLAB = 256
LANES = 128
VMEM_LIMIT = 60 * 1024 * 1024


def _params(semantics):
    return pltpu.CompilerParams(dimension_semantics=semantics,
                                vmem_limit_bytes=VMEM_LIMIT)


def _rms(x, g):
    return x * lax.rsqrt(jnp.mean(x * x, axis=-1, keepdims=True) + EPS) * g


def _ffn_body(x_ref, g_ref, wg_ref, wu_ref, wd_ref, *refs, post_norm, keep_resid, n_casts,
              half):
    refs = list(refs)
    pg_ref = refs.pop(0) if post_norm else None
    cast_in = [refs.pop(0) for _ in range(n_casts)]
    o_ref = refs.pop(0)
    emit_h = post_norm and keep_resid
    h_hbm = refs.pop(0) if emit_h else None
    cast_out = [refs.pop(0) for _ in range(n_casts)]
    xn_ref = refs.pop(0)
    if emit_h:
        h_buf, h_sem = refs
    i = pl.program_id(0)
    f = pl.program_id(1)
    tm = x_ref.shape[0]

    def h_copy(tile):
        return pltpu.make_async_copy(h_buf, h_hbm.at[pl.ds(tile * tm, tm), :], h_sem)

    slabs = [slice(r, r + ROW_SLAB) for r in range(0, tm, ROW_SLAB)]

    @pl.when(f == 0)
    def _():
        for rows in slabs:
            x = x_ref[rows, :]
            xn_ref[rows, :] = _rms(x, g_ref[...]).astype(BF16)
            o_ref[rows, :] = x

    for src_ref, dst_ref in zip(cast_in, cast_out):
        dst_ref[...] = src_ref[...].astype(BF16)

    for r in range(0, x_ref.shape[0], half):
        rows = slice(r, r + half)
        xn = xn_ref[rows, :]
        gate = jnp.dot(xn, wg_ref[...], preferred_element_type=F32)
        up = jnp.dot(xn, wu_ref[...], preferred_element_type=F32)
        hid = (jax.nn.silu(gate) * up * 0.5).astype(BF16)
        o_ref[rows, :] += jnp.dot(hid, wd_ref[...], preferred_element_type=F32)

    @pl.when(f == pl.num_programs(1) - 1)
    def _():
        if emit_h:
            @pl.when(i > 0)
            def _():
                h_copy(i - 1).wait()

        if post_norm:
            for rows in slabs:
                yn = _rms(o_ref[rows, :], pg_ref[...])
                if emit_h:
                    h_buf[rows, :] = yn.astype(BF16)
                else:
                    o_ref[rows, :] = yn

        if emit_h:
            h_copy(i).start()

            @pl.when(i == pl.num_programs(0) - 1)
            def _():
                h_copy(i).wait()


def _ffn(x, g, wg, wu, wd, post_g=None, casts=(), *, keep_resid, tm=1024, tf=512, half=512):
    assert keep_resid or post_g is not None
    emit_h = keep_resid and post_g is not None
    n, d = x.shape
    dff = wg.shape[1]
    ni, nf = n // tm, dff // tf
    row = lambda i, f: (i, 0)
    vec = pl.BlockSpec((1, d), lambda i, f: (0, 0))
    in_specs = [pl.BlockSpec((tm, d), row), vec,
                pl.BlockSpec((d, tf), lambda i, f: (0, f)),
                pl.BlockSpec((d, tf), lambda i, f: (0, f)),
                pl.BlockSpec((tf, d), lambda i, f: (f, 0))]
    args = [x, g, wg, wu, wd]
    if post_g is not None:
        in_specs.append(vec)
        args.append(post_g)
    cast_specs = []
    for a in casts:
        rws, cls = a.shape
        if cls % nf == 0 and (cls // nf) % LANES == 0 and rws % ni == 0:
            cast_specs.append(pl.BlockSpec((rws // ni, cls // nf), lambda i, f: (i, f)))
        else:
            assert rws % (ni * nf) == 0
            cast_specs.append(pl.BlockSpec((rws // (ni * nf), cls),
                                           lambda i, f: (i * nf + f, 0)))
    out_specs = [pl.BlockSpec((tm, d), row)]
    out_shape = [jax.ShapeDtypeStruct((n, d), F32)]
    scratch = [pltpu.VMEM((tm, d), BF16)]
    if emit_h:
        out_specs.append(pl.BlockSpec(memory_space=pl.ANY))
        out_shape.append(jax.ShapeDtypeStruct((n, d), BF16))
        scratch += [pltpu.VMEM((tm, d), BF16), pltpu.SemaphoreType.DMA(())]
    return pl.pallas_call(
        functools.partial(_ffn_body, post_norm=post_g is not None, keep_resid=keep_resid,
                          n_casts=len(casts), half=half),
        grid=(ni, nf),
        in_specs=in_specs + cast_specs,
        out_specs=out_specs + cast_specs,
        out_shape=out_shape + [jax.ShapeDtypeStruct(a.shape, BF16) for a in casts],
        scratch_shapes=scratch,
        compiler_params=_params(("arbitrary", "arbitrary")),
        name="ffn",
    )(*args, *casts)


def _regroup_body(a_ref, b_ref, o_ref, cut_ref, *, cut_blocks, gap):
    c = pl.program_id(0)

    @pl.when(c < cut_blocks)
    def _():
        o_ref[...] = a_ref[...].T.astype(BF16)

    @pl.when(c == cut_blocks - 1)
    def _():
        pad = jnp.zeros((cut_ref.shape[1] - gap, b_ref.shape[1]), F32)
        cut_ref[...] = jnp.concatenate([b_ref[...], pad], axis=0).T

    @pl.when(c >= cut_blocks)
    def _():
        rows = jnp.concatenate([a_ref[gap:, :], b_ref[...]], axis=0)
        o_ref[...] = rows.T.astype(BF16)


def _regroup(wt, *, cut_lo, gap, tc=512):
    width, d = wt.shape
    out_w = width - gap
    assert cut_lo % tc == 0 and cut_lo >= tc and out_w % tc == 0 and tc % gap == 0
    return pl.pallas_call(
        functools.partial(_regroup_body, cut_blocks=cut_lo // tc, gap=gap),
        grid=(out_w // tc,),
        in_specs=[pl.BlockSpec((tc, d), lambda c: (c, 0)),
                  pl.BlockSpec((gap, d), lambda c: ((c + 1) * (tc // gap), 0))],
        out_specs=[pl.BlockSpec((d, tc), lambda c: (0, c)),
                   pl.BlockSpec((d, LANES), lambda c: (0, 0))],
        out_shape=[jax.ShapeDtypeStruct((d, out_w), BF16),
                   jax.ShapeDtypeStruct((d, LANES), F32)],
        compiler_params=_params(("arbitrary",)),
        name="regroup",
    )(wt, wt)


def _matmul_body(x_ref, w_ref, o_ref, *, precision):
    o_ref[...] = jnp.dot(x_ref[...], w_ref[...], preferred_element_type=F32,
                         precision=precision).astype(o_ref.dtype)


def _matmul(x, w, out_dtype, *, tm, tn, name, precision=None):
    n, d = x.shape
    nc = w.shape[1]
    return pl.pallas_call(
        functools.partial(_matmul_body, precision=precision),
        grid=(n // tm, nc // tn),
        in_specs=[pl.BlockSpec((tm, d), lambda i, j: (i, 0)),
                  pl.BlockSpec((d, tn), lambda i, j: (0, j))],
        out_specs=pl.BlockSpec((tm, tn), lambda i, j: (i, j)),
        out_shape=jax.ShapeDtypeStruct((n, nc), out_dtype),
        compiler_params=_params(("parallel", "parallel")),
        name=name,
    )(x, w)


def _norm_matmul_body(x_ref, g_ref, w_ref, o_ref):
    xn = _rms(x_ref[...], g_ref[...]).astype(BF16)
    o_ref[...] = jnp.dot(xn, w_ref[...], preferred_element_type=F32).astype(o_ref.dtype)


def _norm_matmul(x, g, w, out_dtype, *, tm, name):
    n, d = x.shape
    nc = w.shape[1]
    return pl.pallas_call(
        _norm_matmul_body,
        grid=(n // tm,),
        in_specs=[pl.BlockSpec((tm, d), lambda i: (i, 0)),
                  pl.BlockSpec((1, d), lambda i: (0, 0)),
                  pl.BlockSpec((d, nc), lambda i: (0, 0))],
        out_specs=pl.BlockSpec((tm, nc), lambda i: (i, 0)),
        out_shape=jax.ShapeDtypeStruct((n, nc), out_dtype),
        compiler_params=_params(("parallel",)),
        name=name,
    )(x, g, w)


def _la_body(x_ref, w_ref, b_ref, *refs, n_casts):
    cast_in, o_ref, cast_out = refs[:n_casts], refs[n_casts], refs[n_casts + 1:]
    z = jnp.dot(x_ref[...], w_ref[...], preferred_element_type=F32) + b_ref[...]
    o_ref[...] = jax.nn.log_sigmoid(z) * (1.0 / GLA_TAU)
    for src_ref, dst_ref in zip(cast_in, cast_out):
        dst_ref[...] = src_ref[...].astype(BF16)


def _la_proj(x, w, bias, casts=(), *, tm=1024):
    n, d = x.shape
    nc = w.shape[1]
    ni = n // tm
    assert all(a.shape[0] % ni == 0 for a in casts)
    cast_specs = [pl.BlockSpec((a.shape[0] // ni, a.shape[1]), lambda i: (i, 0))
                  for a in casts]
    return pl.pallas_call(
        functools.partial(_la_body, n_casts=len(casts)),
        grid=(ni,),
        in_specs=[pl.BlockSpec((tm, d), lambda i: (i, 0)),
                  pl.BlockSpec((d, nc), lambda i: (0, 0)),
                  pl.BlockSpec((1, nc), lambda i: (0, 0))] + cast_specs,
        out_specs=[pl.BlockSpec((tm, nc), lambda i: (i, 0))] + cast_specs,
        out_shape=[jax.ShapeDtypeStruct((n, nc), F32)]
                  + [jax.ShapeDtypeStruct(a.shape, BF16) for a in casts],
        compiler_params=_params(("parallel",)),
        name="la_proj",
    )(x, w, bias, *casts)


def _gla_body(q_ref, k_ref, v_ref, r_ref, la_ref, on_ref, y_ref, st_ref, kb_ref,
              *, heads, chunk, sub):
    ts = q_ref.shape[0]
    dk = q_ref.shape[1] // heads
    dv = v_ref.shape[1] // heads
    scale = dk ** -0.5

    @pl.when(pl.program_id(1) == 0)
    def _():
        st_ref[...] = jnp.zeros_like(st_ref)

    t_idx = lax.broadcasted_iota(jnp.int32, (chunk, chunk), 0)
    s_idx = lax.broadcasted_iota(jnp.int32, (chunk, chunk), 1)
    causal = t_idx >= s_idx
    ones_tril = causal.astype(BF16)
    lane_in_sub = lax.broadcasted_iota(jnp.int32, (sub, chunk), 1) % sub

    halves = []
    m = chunk // 2
    while m >= sub:
        halves.append(m)
        m //= 2
    lane_half = {m: lax.broadcasted_iota(jnp.int32, (m, chunk), 1) >> (m.bit_length() - 1)
                 for m in halves}

    for c in range(ts // chunk):
        rows = slice(c * chunk, (c + 1) * chunk)
        kb = kb_ref.at[c % 2]
        la = la_ref[rows, :] * LOG2_E
        la_hi = la.astype(BF16)
        la_lo = (la - la_hi.astype(F32)).astype(BF16)
        b_all = (jnp.dot(ones_tril, la_hi, preferred_element_type=F32)
                 + jnp.dot(ones_tril, la_lo, preferred_element_type=F32))
        for h in range(heads):
            kcols = slice(h * dk, (h + 1) * dk)
            vcols = slice(h * dv, (h + 1) * dv)
            b = b_all[:, kcols]
            q = q_ref[rows, kcols].astype(F32) * scale
            k = k_ref[rows, kcols].astype(F32)
            v = v_ref[rows, vcols]
            st = st_ref[h]
            inter = lax.dot_general((q * jnp.exp2(b)).astype(BF16), st.astype(BF16),
                                    NT_DIMS, preferred_element_type=F32)

            kb[h, 0] = k
            kb[h, 1] = b
            diag = []
            for i in range(chunk // sub):
                lo = i * sub
                qi, bi = q[lo:lo + sub], b[lo:lo + sub]
                d = jnp.zeros((sub, chunk), F32)
                for s in range(sub):
                    ks = jnp.broadcast_to(kb[h, 0, lo + s:lo + s + 1, :], (sub, dk))
                    bs = jnp.broadcast_to(kb[h, 1, lo + s:lo + s + 1, :], (sub, dk))
                    col = jnp.sum(qi * ks * jnp.exp2(bi - bs), axis=-1, keepdims=True)
                    d = jnp.where(lane_in_sub == s, col, d)
                diag.append(d)
            scores = jnp.concatenate(diag, axis=0)

            for m in halves:
                parts = []
                for j in range(0, chunk, 2 * m):
                    b_mid = b[j + m - 1:j + m]
                    parts.append(k[j:j + m] * jnp.exp2(b_mid - b[j:j + m]))
                    parts.append(q[j + m:j + 2 * m] * jnp.exp2(b[j + m:j + 2 * m] - b_mid))
                z = jnp.concatenate(parts, axis=0).astype(BF16)
                pair = lax.dot_general(z, z, NT_DIMS, preferred_element_type=F32)
                out = []
                for j in range(0, chunk, 2 * m):
                    out.append(scores[j:j + m])
                    out.append(jnp.where(lane_half[m] == j // m, pair[j + m:j + 2 * m],
                                         scores[j + m:j + 2 * m]))
                scores = jnp.concatenate(out, axis=0)
            scores = jnp.where(causal, scores, 0.0)
            o = inter + jnp.dot(scores.astype(BF16), v, preferred_element_type=F32)

            b_end = b[chunk - 1:chunk]
            k_end = (k * jnp.exp2(b_end - b)).astype(BF16)
            upd = lax.dot_general(v, k_end, TN_DIMS, preferred_element_type=F32)
            st_ref[h] = st * jnp.exp2(b_end) + upd

            o = o * lax.rsqrt(jnp.mean(o * o, axis=-1, keepdims=True) + EPS)
            o = o * on_ref[:, vcols] * jax.nn.silu(r_ref[rows, vcols].astype(F32))
            y_ref[rows, vcols] = o.astype(y_ref.dtype)


def _gla(proj, la, onorm, *, batch, seq, ts=512):
    hk = la.shape[1]
    hv = onorm.shape[1]
    ns = seq // ts
    row = lambda b, s: b * ns + s
    return pl.pallas_call(
        functools.partial(_gla_body, heads=GLA_HEADS, chunk=GLA_CHUNK, sub=GLA_SUB),
        grid=(batch, ns),
        in_specs=[
            pl.BlockSpec((ts, hk), lambda b, s: (row(b, s), 0)),
            pl.BlockSpec((ts, hk), lambda b, s: (row(b, s), 1)),
            pl.BlockSpec((ts, hv), lambda b, s: (row(b, s), hk * 2 // hv)),
            pl.BlockSpec((ts, hv), lambda b, s: (row(b, s), hk * 2 // hv + 1)),
            pl.BlockSpec((ts, hk), lambda b, s: (row(b, s), 0)),
            pl.BlockSpec((1, hv), lambda b, s: (0, 0)),
        ],
        out_specs=pl.BlockSpec((ts, hv), lambda b, s: (row(b, s), 0)),
        out_shape=jax.ShapeDtypeStruct((batch * seq, hv), BF16),
        scratch_shapes=[pltpu.VMEM((GLA_HEADS, hv // GLA_HEADS, hk // GLA_HEADS), F32),
                        pltpu.VMEM((2, GLA_HEADS, 2, GLA_CHUNK, hk // GLA_HEADS), F32)],
        compiler_params=_params(("parallel", "arbitrary")),
        name="gla",
    )(proj, proj, proj, proj, la, onorm)


def _gelu(x):
    return 0.5 * x * (1.0 + lax.erf(x * (2.0 ** -0.5)))


def _sg_body(u_ref, v_ref, lg_ref, lb_ref, ws_ref, bs_ref, y_ref, *, groups, chunk):
    ts = u_ref.shape[0]
    gdim = u_ref.shape[1] // groups
    rr = lax.broadcasted_iota(jnp.int32, (chunk, chunk), 0)
    cc = lax.broadcasted_iota(jnp.int32, (chunk, chunk), 1)
    for g in range(groups):
        cols = slice(g * gdim, (g + 1) * gdim)
        w = jnp.where(rr >= cc, ws_ref[g], 0.0).astype(BF16)
        for c in range(ts // chunk):
            rows = slice(c * chunk, (c + 1) * chunk)
            u = _gelu(u_ref[rows, cols].astype(F32))
            v = _gelu(v_ref[rows, cols].astype(F32))
            mu = jnp.mean(v, axis=-1, keepdims=True)
            var = jnp.mean((v - mu) ** 2, axis=-1, keepdims=True)
            vn = (v - mu) * lax.rsqrt(var + EPS) * lg_ref[g:g + 1, :] + lb_ref[g:g + 1, :]
            vs = jnp.dot(w, vn.astype(BF16), preferred_element_type=F32) + bs_ref[g]
            y_ref[rows, cols] = (u * vs).astype(y_ref.dtype)


def _xa_body(q_ref, k_ref, v_ref, y_ref, *, heads):
    dh = q_ref.shape[1] // heads
    for h in range(heads):
        cols = slice(h * dh, (h + 1) * dh)
        s = lax.dot_general(q_ref[:, cols], k_ref[:, cols], NT_DIMS,
                            preferred_element_type=F32) * (dh ** -0.5)
        p = jnp.exp(s - jnp.max(s, axis=-1, keepdims=True))
        p = p / jnp.sum(p, axis=-1, keepdims=True)
        y_ref[:, cols] = jnp.dot(p.astype(BF16), v_ref[:, cols],
                                 preferred_element_type=F32).astype(y_ref.dtype)


def _merge_body(su0_ref, sv0_ref, xq0_ref, mk0_ref, mv0_ref,
                su_ref, sv_ref, xq_ref, mk_ref, mv_ref,
                ya_ref, ga_ref, gb_ref, gc_ref, lg_ref, lb_ref, ws_ref, bs_ref,
                wb_ref, wo_ref, x_ref, o_ref,
                yb_cur, yc_cur, yb_nxt, yc_nxt, *, groups, chunk, heads):
    def branches(u_ref, v_ref, q_ref, k_ref, w_ref, yb_ref, yc_ref):
        _sg_body(u_ref, v_ref, lg_ref, lb_ref, ws_ref, bs_ref, yb_ref, groups=groups, chunk=chunk)
        _xa_body(q_ref, k_ref, w_ref, yc_ref, heads=heads)

    @pl.when(pl.program_id(0) == 0)
    def _():
        branches(su0_ref, sv0_ref, xq0_ref, mk0_ref, mv0_ref, yb_cur, yc_cur)

    merged = None
    for i, (y_ref, g_ref) in enumerate(((ya_ref, ga_ref), (yb_cur, gb_ref), (yc_cur, gc_ref))):
        t = jnp.dot(y_ref[...], wb_ref[i], preferred_element_type=F32)
        t = jax.nn.sigmoid(g_ref[...].astype(F32)) * t
        merged = t if merged is None else merged + t
    o_ref[...] = x_ref[...] + jnp.dot(merged.astype(BF16), wo_ref[...],
                                      preferred_element_type=F32)

    branches(su_ref, sv_ref, xq_ref, mk_ref, mv_ref, yb_nxt, yc_nxt)
    yb_cur[...] = yb_nxt[...]
    yc_cur[...] = yc_nxt[...]


def _merge(ya, proj, kv, ln_g, ln_b, w_s, b_s, wb, wo, x, *, u_block, q_block, gate_block,
           seq, tm=256):
    n, d = x.shape
    bw = ya.shape[1]
    groups, gdim = ln_g.shape
    chunk = w_s.shape[1]
    mem_len = kv.shape[0] // (n // seq)
    tiles_per_seq = seq // tm
    last = n // tm - 1
    assert tm % chunk == 0 and groups * gdim == bw and kv.shape[1] == 2 * bw
    row = lambda i: (i, 0)
    nxt = lambda i: jnp.minimum(i + 1, last)
    const2 = lambda i: (0, 0)
    const3 = lambda i: (0, 0, 0)
    once = pl.Buffered(1)
    tile = lambda f: pl.BlockSpec((tm, bw), f)
    mem = lambda f: pl.BlockSpec((mem_len, bw), f)
    return pl.pallas_call(
        functools.partial(_merge_body, groups=groups, chunk=chunk, heads=XA_HEADS),
        grid=(n // tm,),
        in_specs=[
            tile(lambda i: (0, u_block)), tile(lambda i: (0, u_block + 1)),
            tile(lambda i: (0, q_block)),
            mem(lambda i: (0, 0)), mem(lambda i: (0, 1)),
            tile(lambda i: (nxt(i), u_block)), tile(lambda i: (nxt(i), u_block + 1)),
            tile(lambda i: (nxt(i), q_block)),
            mem(lambda i: (nxt(i) // tiles_per_seq, 0)), mem(lambda i: (nxt(i) // tiles_per_seq, 1)),
            tile(row),
            pl.BlockSpec((tm, d), lambda i: (i, gate_block)),
            pl.BlockSpec((tm, d), lambda i: (i, gate_block + 1)),
            pl.BlockSpec((tm, d), lambda i: (i, gate_block + 2)),
            pl.BlockSpec((groups, gdim), const2),
            pl.BlockSpec((groups, gdim), const2),
            pl.BlockSpec((groups, chunk, chunk), const3),
            pl.BlockSpec((groups, chunk, 1), const3),
            pl.BlockSpec(wb.shape, const3, pipeline_mode=once),
            pl.BlockSpec(wo.shape, const2, pipeline_mode=once),
            pl.BlockSpec((tm, d), row),
        ],
        out_specs=pl.BlockSpec((tm, d), row),
        out_shape=jax.ShapeDtypeStruct((n, d), F32),
        scratch_shapes=[pltpu.VMEM((tm, bw), BF16)] * 4,
        compiler_params=_params(("arbitrary",)),
        name="merge",
    )(proj, proj, proj, kv, kv, proj, proj, proj, kv, kv, ya, proj, proj, proj,
      ln_g, ln_b, w_s, b_s, wb, wo, x)


def kernel(x, mem, ffn1_norm, ffn1_w_gate, ffn1_w_up, ffn1_w_down, mix_norm, mem_norm, w_in, gla_w_gate_up, gla_gate_bias, gla_out_norm, sg_ln_g, sg_ln_b, sg_w_s, sg_b_s, w_kv_mem, w_branch, w_out, ffn2_norm, ffn2_w_gate, ffn2_w_up, ffn2_w_down, final_norm):
    batch, seq, d = x.shape
    depth = w_in.shape[0]
    hk = gla_w_gate_up.shape[2]
    hv = gla_out_norm.shape[1] * gla_out_norm.shape[2]
    bw = d // 2
    rank = gla_w_gate_up.shape[1]
    lr_lo = 2 * hk + 2 * hv

    xf = x.reshape(batch * seq, d)
    memf = mem.reshape(-1, d)
    for l in range(depth):
        last = l == depth - 1
        x1, h, wg2, wu2, wd2 = _ffn(
            xf, ffn1_norm[l][None], ffn1_w_gate[l].astype(BF16), ffn1_w_up[l].astype(BF16),
            ffn1_w_down[l].astype(BF16), post_g=mix_norm[l][None],
            casts=(ffn2_w_gate[l], ffn2_w_up[l], ffn2_w_down[l]), keep_resid=True)

        w_in_t = jnp.swapaxes(w_in[l], 0, 1)
        w_main, w_lr = _regroup(w_in_t, cut_lo=lr_lo, gap=rank)
        proj = _matmul(h, w_main, BF16, tm=1024, tn=2048, name="proj")
        w_up = jnp.pad(gla_w_gate_up[l], ((0, LANES - rank), (0, 0)))
        w_la = _matmul(w_lr, w_up, BF16, tm=d, tn=hk, name="fold_gate", precision=HIGHEST)
        la, wb, wo, wkv = _la_proj(h, w_la, gla_gate_bias[l][None],
                                   casts=(w_branch[l].reshape(-1, d), w_out[l], w_kv_mem[l]))
        kv = _norm_matmul(memf, mem_norm[l][None], wkv, BF16, tm=512, name="mem_kv")

        y_gla = _gla(proj, la, gla_out_norm[l].reshape(1, hv), batch=batch, seq=seq)
        x2 = _merge(y_gla, proj, kv, sg_ln_g[l], sg_ln_b[l], sg_w_s[l], sg_b_s[l][:, :, None],
                    wb.reshape(w_branch[l].shape), wo, x1,
                    u_block=lr_lo // bw, q_block=lr_lo // bw + 2,
                    gate_block=(lr_lo + 3 * bw) // d, seq=seq)

        (xf,) = _ffn(x2, ffn2_norm[l][None], wg2, wu2, wd2,
                     post_g=final_norm[None] if last else None, keep_resid=not last)
    return xf.reshape(batch, seq, d)
```

```python
import functools

import jax
import jax.numpy as jnp
from jax import lax
from jax.experimental import pallas as pl
from jax.experimental.pallas import tpu as pltpu

F32 = jnp.float32
BF16 = jnp.bfloat16
HIGHEST = lax.Precision.HIGHEST
NT_DIMS = (((1,), (1,)), ((), ()))
TN_DIMS = (((0,), (0,)), ((), ()))

EPS = 1e-6
LOG2_E = 1.4426950408889634
GLA_HEADS = 4
GLA_TAU = 16.0
GLA_CHUNK = 128
GLA_SUB = 8
XA_HEADS = 4

ROW_SLAB = 256
LANES = 128
VMEM_LIMIT = 60 * 1024 * 1024


def _params(semantics):
    return pltpu.CompilerParams(dimension_semantics=semantics,
                                vmem_limit_bytes=VMEM_LIMIT)


def _rms(x, g):
    return x * lax.rsqrt(jnp.mean(x * x, axis=-1, keepdims=True) + EPS) * g


def _ffn_body(x_ref, g_ref, wg_ref, wu_ref, wd_ref, *refs, post_norm, keep_resid, n_casts,
              half):
    refs = list(refs)
    pg_ref = refs.pop(0) if post_norm else None
    cast_in = [refs.pop(0) for _ in range(n_casts)]
    o_ref = refs.pop(0)
    emit_h = post_norm and keep_resid
    h_hbm = refs.pop(0) if emit_h else None
    cast_out = [refs.pop(0) for _ in range(n_casts)]
    xn_ref = refs.pop(0)
    if emit_h:
        h_buf, h_sem = refs
    i = pl.program_id(0)
    f = pl.program_id(1)
    tm = x_ref.shape[0]

    def h_copy(tile):
        return pltpu.make_async_copy(h_buf, h_hbm.at[pl.ds(tile * tm, tm), :], h_sem)

    slabs = [slice(r, r + ROW_SLAB) for r in range(0, tm, ROW_SLAB)]

    @pl.when(f == 0)
    def _():
        for rows in slabs:
            x = x_ref[rows, :]
            xn_ref[rows, :] = _rms(x, g_ref[...]).astype(BF16)
            o_ref[rows, :] = x

    for src_ref, dst_ref in zip(cast_in, cast_out):
        dst_ref[...] = src_ref[...].astype(BF16)

    for r in range(0, tm, half):
        rows = slice(r, r + half)
        xn = xn_ref[rows, :]
        gate = jnp.dot(xn, wg_ref[...], preferred_element_type=F32)
        up = jnp.dot(xn, wu_ref[...], preferred_element_type=F32)
        hid = (jax.nn.silu(gate) * up * 0.5).astype(BF16)
        o_ref[rows, :] += jnp.dot(hid, wd_ref[...], preferred_element_type=F32)

    @pl.when(f == pl.num_programs(1) - 1)
    def _():
        if emit_h:
            @pl.when(i > 0)
            def _():
                h_copy(i - 1).wait()

        if post_norm:
            for rows in slabs:
                yn = _rms(o_ref[rows, :], pg_ref[...])
                if emit_h:
                    h_buf[rows, :] = yn.astype(BF16)
                else:
                    o_ref[rows, :] = yn

        if emit_h:
            h_copy(i).start()

            @pl.when(i == pl.num_programs(0) - 1)
            def _():
                h_copy(i).wait()


def _ffn(x, g, wg, wu, wd, post_g=None, casts=(), *, keep_resid, tm=1024, tf=512, half=512):
    assert keep_resid or post_g is not None
    emit_h = keep_resid and post_g is not None
    n, d = x.shape
    dff = wg.shape[1]
    ni, nf = n // tm, dff // tf
    row = lambda i, f: (i, 0)
    vec = pl.BlockSpec((1, d), lambda i, f: (0, 0))
    in_specs = [pl.BlockSpec((tm, d), row), vec,
                pl.BlockSpec((d, tf), lambda i, f: (0, f)),
                pl.BlockSpec((d, tf), lambda i, f: (0, f)),
                pl.BlockSpec((tf, d), lambda i, f: (f, 0))]
    args = [x, g, wg, wu, wd]
    if post_g is not None:
        in_specs.append(vec)
        args.append(post_g)
    cast_specs = []
    for a in casts:
        rws, cls = a.shape
        if cls % nf == 0 and (cls // nf) % LANES == 0 and rws % ni == 0:
            cast_specs.append(pl.BlockSpec((rws // ni, cls // nf), lambda i, f: (i, f)))
        else:
            assert rws % (ni * nf) == 0
            cast_specs.append(pl.BlockSpec((rws // (ni * nf), cls),
                                           lambda i, f: (i * nf + f, 0)))
    out_specs = [pl.BlockSpec((tm, d), row)]
    out_shape = [jax.ShapeDtypeStruct((n, d), F32)]
    scratch = [pltpu.VMEM((tm, d), BF16)]
    if emit_h:
        out_specs.append(pl.BlockSpec(memory_space=pl.ANY))
        out_shape.append(jax.ShapeDtypeStruct((n, d), BF16))
        scratch += [pltpu.VMEM((tm, d), BF16), pltpu.SemaphoreType.DMA(())]
    return pl.pallas_call(
        functools.partial(_ffn_body, post_norm=post_g is not None, keep_resid=keep_resid,
                          n_casts=len(casts), half=half),
        grid=(ni, nf),
        in_specs=in_specs + cast_specs,
        out_specs=out_specs + cast_specs,
        out_shape=out_shape + [jax.ShapeDtypeStruct(a.shape, BF16) for a in casts],
        scratch_shapes=scratch,
        compiler_params=_params(("arbitrary", "arbitrary")),
        name="ffn",
    )(*args, *casts)


def _regroup_body(a_ref, b_ref, o_ref, cut_ref, *, cut_blocks, gap):
    c = pl.program_id(0)

    @pl.when(c < cut_blocks)
    def _():
        o_ref[...] = a_ref[...].T.astype(BF16)

    @pl.when(c == cut_blocks - 1)
    def _():
        pad = jnp.zeros((cut_ref.shape[1] - gap, b_ref.shape[1]), F32)
        cut_ref[...] = jnp.concatenate([b_ref[...], pad], axis=0).T

    @pl.when(c >= cut_blocks)
    def _():
        rows = jnp.concatenate([a_ref[gap:, :], b_ref[...]], axis=0)
        o_ref[...] = rows.T.astype(BF16)


def _regroup(wt, *, cut_lo, gap, tc=512):
    width, d = wt.shape
    out_w = width - gap
    assert cut_lo % tc == 0 and cut_lo >= tc and out_w % tc == 0 and tc % gap == 0
    return pl.pallas_call(
        functools.partial(_regroup_body, cut_blocks=cut_lo // tc, gap=gap),
        grid=(out_w // tc,),
        in_specs=[pl.BlockSpec((tc, d), lambda c: (c, 0)),
                  pl.BlockSpec((gap, d), lambda c: ((c + 1) * (tc // gap), 0))],
        out_specs=[pl.BlockSpec((d, tc), lambda c: (0, c)),
                   pl.BlockSpec((d, LANES), lambda c: (0, 0))],
        out_shape=[jax.ShapeDtypeStruct((d, out_w), BF16),
                   jax.ShapeDtypeStruct((d, LANES), F32)],
        compiler_params=_params(("arbitrary",)),
        name="regroup",
    )(wt, wt)


def _matmul_body(x_ref, w_ref, o_ref, *, precision):
    o_ref[...] = jnp.dot(x_ref[...], w_ref[...], preferred_element_type=F32,
                         precision=precision).astype(o_ref.dtype)


def _matmul(x, w, out_dtype, *, tm, tn, name, precision=None):
    n, d = x.shape
    nc = w.shape[1]
    return pl.pallas_call(
        functools.partial(_matmul_body, precision=precision),
        grid=(n // tm, nc // tn),
        in_specs=[pl.BlockSpec((tm, d), lambda i, j: (i, 0)),
                  pl.BlockSpec((d, tn), lambda i, j: (0, j))],
        out_specs=pl.BlockSpec((tm, tn), lambda i, j: (i, j)),
        out_shape=jax.ShapeDtypeStruct((n, nc), out_dtype),
        compiler_params=_params(("parallel", "parallel")),
        name=name,
    )(x, w)


def _norm_matmul_body(x_ref, g_ref, w_ref, o_ref):
    xn = _rms(x_ref[...], g_ref[...]).astype(BF16)
    o_ref[...] = jnp.dot(xn, w_ref[...], preferred_element_type=F32).astype(o_ref.dtype)


def _norm_matmul(x, g, w, out_dtype, *, tm, name):
    n, d = x.shape
    nc = w.shape[1]
    return pl.pallas_call(
        _norm_matmul_body,
        grid=(n // tm,),
        in_specs=[pl.BlockSpec((tm, d), lambda i: (i, 0)),
                  pl.BlockSpec((1, d), lambda i: (0, 0)),
                  pl.BlockSpec((d, nc), lambda i: (0, 0))],
        out_specs=pl.BlockSpec((tm, nc), lambda i: (i, 0)),
        out_shape=jax.ShapeDtypeStruct((n, nc), out_dtype),
        compiler_params=_params(("parallel",)),
        name=name,
    )(x, g, w)


def _la_body(x_ref, w_ref, b_ref, *refs, n_casts):
    cast_in, o_ref, cast_out = refs[:n_casts], refs[n_casts], refs[n_casts + 1:]
    z = jnp.dot(x_ref[...], w_ref[...], preferred_element_type=F32) + b_ref[...]
    o_ref[...] = jax.nn.log_sigmoid(z) * (1.0 / GLA_TAU)
    for src_ref, dst_ref in zip(cast_in, cast_out):
        dst_ref[...] = src_ref[...].astype(BF16)


def _la_proj(x, w, bias, casts=(), *, tm=1024):
    n, d = x.shape
    nc = w.shape[1]
    ni = n // tm
    assert all(a.shape[0] % ni == 0 for a in casts)
    cast_specs = [pl.BlockSpec((a.shape[0] // ni, a.shape[1]), lambda i: (i, 0))
                  for a in casts]
    return pl.pallas_call(
        functools.partial(_la_body, n_casts=len(casts)),
        grid=(ni,),
        in_specs=[pl.BlockSpec((tm, d), lambda i: (i, 0)),
                  pl.BlockSpec((d, nc), lambda i: (0, 0)),
                  pl.BlockSpec((1, nc), lambda i: (0, 0))] + cast_specs,
        out_specs=[pl.BlockSpec((tm, nc), lambda i: (i, 0))] + cast_specs,
        out_shape=[jax.ShapeDtypeStruct((n, nc), F32)]
                  + [jax.ShapeDtypeStruct(a.shape, BF16) for a in casts],
        compiler_params=_params(("parallel",)),
        name="la_proj",
    )(x, w, bias, *casts)


def _gla_body(q_ref, k_ref, v_ref, r_ref, la_ref, on_ref, y_ref, st_ref, kb_ref,
              *, heads, chunk, sub):
    ts = q_ref.shape[0]
    dk = q_ref.shape[1] // heads
    dv = v_ref.shape[1] // heads
    scale = dk ** -0.5

    @pl.when(pl.program_id(1) == 0)
    def _():
        st_ref[...] = jnp.zeros_like(st_ref)

    t_idx = lax.broadcasted_iota(jnp.int32, (chunk, chunk), 0)
    s_idx = lax.broadcasted_iota(jnp.int32, (chunk, chunk), 1)
    causal = t_idx >= s_idx
    ones_tril = causal.astype(BF16)
    lane_in_sub = lax.broadcasted_iota(jnp.int32, (sub, chunk), 1) % sub

    halves = []
    m = chunk // 2
    while m >= sub:
        halves.append(m)
        m //= 2
    lane_half = {m: lax.broadcasted_iota(jnp.int32, (m, chunk), 1) >> (m.bit_length() - 1)
                 for m in halves}

    for c in range(ts // chunk):
        rows = slice(c * chunk, (c + 1) * chunk)
        kb = kb_ref.at[c % 2]
        la = la_ref[rows, :] * LOG2_E
        la_hi = la.astype(BF16)
        la_lo = (la - la_hi.astype(F32)).astype(BF16)
        b_all = (jnp.dot(ones_tril, la_hi, preferred_element_type=F32)
                 + jnp.dot(ones_tril, la_lo, preferred_element_type=F32))
        for h in range(heads):
            kcols = slice(h * dk, (h + 1) * dk)
            vcols = slice(h * dv, (h + 1) * dv)
            b = b_all[:, kcols]
            q = q_ref[rows, kcols].astype(F32) * scale
            k = k_ref[rows, kcols].astype(F32)
            v = v_ref[rows, vcols]
            st = st_ref[h]
            inter = lax.dot_general((q * jnp.exp2(b)).astype(BF16), st.astype(BF16),
                                    NT_DIMS, preferred_element_type=F32)

            kb[h, 0] = k
            kb[h, 1] = b
            diag = []
            for i in range(chunk // sub):
                lo = i * sub
                qi, bi = q[lo:lo + sub], b[lo:lo + sub]
                d = jnp.zeros((sub, chunk), F32)
                for s in range(sub):
                    ks = jnp.broadcast_to(kb[h, 0, lo + s:lo + s + 1, :], (sub, dk))
                    bs = jnp.broadcast_to(kb[h, 1, lo + s:lo + s + 1, :], (sub, dk))
                    col = jnp.sum(qi * ks * jnp.exp2(bi - bs), axis=-1, keepdims=True)
                    d = jnp.where(lane_in_sub == s, col, d)
                diag.append(d)
            scores = jnp.concatenate(diag, axis=0)

            for m in halves:
                parts = []
                for j in range(0, chunk, 2 * m):
                    b_mid = b[j + m - 1:j + m]
                    parts.append(k[j:j + m] * jnp.exp2(b_mid - b[j:j + m]))
                    parts.append(q[j + m:j + 2 * m] * jnp.exp2(b[j + m:j + 2 * m] - b_mid))
                z = jnp.concatenate(parts, axis=0).astype(BF16)
                pair = lax.dot_general(z, z, NT_DIMS, preferred_element_type=F32)
                out = []
                for j in range(0, chunk, 2 * m):
                    out.append(scores[j:j + m])
                    out.append(jnp.where(lane_half[m] == j // m, pair[j + m:j + 2 * m],
                                         scores[j + m:j + 2 * m]))
                scores = jnp.concatenate(out, axis=0)
            scores = jnp.where(causal, scores, 0.0)
            o = inter + jnp.dot(scores.astype(BF16), v, preferred_element_type=F32)

            b_end = b[chunk - 1:chunk]
            k_end = (k * jnp.exp2(b_end - b)).astype(BF16)
            upd = lax.dot_general(v, k_end, TN_DIMS, preferred_element_type=F32)
            st_ref[h] = st * jnp.exp2(b_end) + upd

            o = o * lax.rsqrt(jnp.mean(o * o, axis=-1, keepdims=True) + EPS)
            o = o * on_ref[:, vcols] * jax.nn.silu(r_ref[rows, vcols].astype(F32))
            y_ref[rows, vcols] = o.astype(y_ref.dtype)


def _gla(proj, la, onorm, *, batch, seq, ts=1024):
    hk = la.shape[1]
    hv = onorm.shape[1]
    ns = seq // ts
    row = lambda b, s: b * ns + s
    return pl.pallas_call(
        functools.partial(_gla_body, heads=GLA_HEADS, chunk=GLA_CHUNK, sub=GLA_SUB),
        grid=(batch, ns),
        in_specs=[
            pl.BlockSpec((ts, hk), lambda b, s: (row(b, s), 0)),
            pl.BlockSpec((ts, hk), lambda b, s: (row(b, s), 1)),
            pl.BlockSpec((ts, hv), lambda b, s: (row(b, s), hk * 2 // hv)),
            pl.BlockSpec((ts, hv), lambda b, s: (row(b, s), hk * 2 // hv + 1)),
            pl.BlockSpec((ts, hk), lambda b, s: (row(b, s), 0)),
            pl.BlockSpec((1, hv), lambda b, s: (0, 0)),
        ],
        out_specs=pl.BlockSpec((ts, hv), lambda b, s: (row(b, s), 0)),
        out_shape=jax.ShapeDtypeStruct((batch * seq, hv), BF16),
        scratch_shapes=[pltpu.VMEM((GLA_HEADS, hv // GLA_HEADS, hk // GLA_HEADS), F32),
                        pltpu.VMEM((2, GLA_HEADS, 2, GLA_CHUNK, hk // GLA_HEADS), F32)],
        compiler_params=_params(("parallel", "arbitrary")),
        name="gla",
    )(proj, proj, proj, proj, la, onorm)


def _gelu(x):
    return 0.5 * x * (1.0 + lax.erf(x * (2.0 ** -0.5)))


def _sg_body(u_ref, v_ref, lg_ref, lb_ref, ws_ref, bs_ref, y_ref, *, groups, chunk):
    ts = u_ref.shape[0]
    gdim = u_ref.shape[1] // groups
    rr = lax.broadcasted_iota(jnp.int32, (chunk, chunk), 0)
    cc = lax.broadcasted_iota(jnp.int32, (chunk, chunk), 1)
    for g in range(groups):
        cols = slice(g * gdim, (g + 1) * gdim)
        w = jnp.where(rr >= cc, ws_ref[g], 0.0).astype(BF16)
        for c in range(ts // chunk):
            rows = slice(c * chunk, (c + 1) * chunk)
            u = _gelu(u_ref[rows, cols].astype(F32))
            v = _gelu(v_ref[rows, cols].astype(F32))
            mu = jnp.mean(v, axis=-1, keepdims=True)
            var = jnp.mean((v - mu) ** 2, axis=-1, keepdims=True)
            vn = (v - mu) * lax.rsqrt(var + EPS) * lg_ref[g:g + 1, :] + lb_ref[g:g + 1, :]
            vs = jnp.dot(w, vn.astype(BF16), preferred_element_type=F32) + bs_ref[g]
            y_ref[rows, cols] = (u * vs).astype(y_ref.dtype)


def _xa_body(q_ref, k_ref, v_ref, y_ref, *, heads):
    dh = q_ref.shape[1] // heads
    for h in range(heads):
        cols = slice(h * dh, (h + 1) * dh)
        s = lax.dot_general(q_ref[:, cols], k_ref[:, cols], NT_DIMS,
                            preferred_element_type=F32) * (dh ** -0.5)
        p = jnp.exp(s - jnp.max(s, axis=-1, keepdims=True))
        p = p / jnp.sum(p, axis=-1, keepdims=True)
        y_ref[:, cols] = jnp.dot(p.astype(BF16), v_ref[:, cols],
                                 preferred_element_type=F32).astype(y_ref.dtype)


def _merge_body(su0_ref, sv0_ref, xq0_ref, mk0_ref, mv0_ref,
                su_ref, sv_ref, xq_ref, mk_ref, mv_ref,
                ya_ref, ga_ref, gb_ref, gc_ref, lg_ref, lb_ref, ws_ref, bs_ref,
                wb_ref, wo_ref, x_ref, o_ref,
                yb_cur, yc_cur, yb_nxt, yc_nxt, *, groups, chunk, heads):
    def branches(u_ref, v_ref, q_ref, k_ref, w_ref, yb_ref, yc_ref):
        _sg_body(u_ref, v_ref, lg_ref, lb_ref, ws_ref, bs_ref, yb_ref, groups=groups, chunk=chunk)
        _xa_body(q_ref, k_ref, w_ref, yc_ref, heads=heads)

    @pl.when(pl.program_id(0) == 0)
    def _():
        branches(su0_ref, sv0_ref, xq0_ref, mk0_ref, mv0_ref, yb_cur, yc_cur)

    merged = None
    for i, (y_ref, g_ref) in enumerate(((ya_ref, ga_ref), (yb_cur, gb_ref), (yc_cur, gc_ref))):
        t = jnp.dot(y_ref[...], wb_ref[i], preferred_element_type=F32)
        t = jax.nn.sigmoid(g_ref[...].astype(F32)) * t
        merged = t if merged is None else merged + t
    o_ref[...] = x_ref[...] + jnp.dot(merged.astype(BF16), wo_ref[...],
                                      preferred_element_type=F32)

    branches(su_ref, sv_ref, xq_ref, mk_ref, mv_ref, yb_nxt, yc_nxt)
    yb_cur[...] = yb_nxt[...]
    yc_cur[...] = yc_nxt[...]


def _merge(ya, proj, kv, ln_g, ln_b, w_s, b_s, wb, wo, x, *, u_block, q_block, gate_block,
           seq, tm=256):
    n, d = x.shape
    bw = ya.shape[1]
    groups, gdim = ln_g.shape
    chunk = w_s.shape[1]
    mem_len = kv.shape[0] // (n // seq)
    tiles_per_seq = seq // tm
    last = n // tm - 1
    assert tm % chunk == 0 and groups * gdim == bw and kv.shape[1] == 2 * bw
    row = lambda i: (i, 0)
    nxt = lambda i: jnp.minimum(i + 1, last)
    const2 = lambda i: (0, 0)
    const3 = lambda i: (0, 0, 0)
    once = pl.Buffered(1)
    tile = lambda f: pl.BlockSpec((tm, bw), f)
    mem = lambda f: pl.BlockSpec((mem_len, bw), f)
    return pl.pallas_call(
        functools.partial(_merge_body, groups=groups, chunk=chunk, heads=XA_HEADS),
        grid=(n // tm,),
        in_specs=[
            tile(lambda i: (0, u_block)), tile(lambda i: (0, u_block + 1)),
            tile(lambda i: (0, q_block)),
            mem(lambda i: (0, 0)), mem(lambda i: (0, 1)),
            tile(lambda i: (nxt(i), u_block)), tile(lambda i: (nxt(i), u_block + 1)),
            tile(lambda i: (nxt(i), q_block)),
            mem(lambda i: (nxt(i) // tiles_per_seq, 0)), mem(lambda i: (nxt(i) // tiles_per_seq, 1)),
            tile(row),
            pl.BlockSpec((tm, d), lambda i: (i, gate_block)),
            pl.BlockSpec((tm, d), lambda i: (i, gate_block + 1)),
            pl.BlockSpec((tm, d), lambda i: (i, gate_block + 2)),
            pl.BlockSpec((groups, gdim), const2),
            pl.BlockSpec((groups, gdim), const2),
            pl.BlockSpec((groups, chunk, chunk), const3),
            pl.BlockSpec((groups, chunk, 1), const3),
            pl.BlockSpec(wb.shape, const3, pipeline_mode=once),
            pl.BlockSpec(wo.shape, const2, pipeline_mode=once),
            pl.BlockSpec((tm, d), row),
        ],
        out_specs=pl.BlockSpec((tm, d), row),
        out_shape=jax.ShapeDtypeStruct((n, d), F32),
        scratch_shapes=[pltpu.VMEM((tm, bw), BF16)] * 4,
        compiler_params=_params(("arbitrary",)),
        name="merge",
    )(proj, proj, proj, kv, kv, proj, proj, proj, kv, kv, ya, proj, proj, proj,
      ln_g, ln_b, w_s, b_s, wb, wo, x)


def kernel(x, mem, ffn1_norm, ffn1_w_gate, ffn1_w_up, ffn1_w_down, mix_norm, mem_norm, w_in, gla_w_gate_up, gla_gate_bias, gla_out_norm, sg_ln_g, sg_ln_b, sg_w_s, sg_b_s, w_kv_mem, w_branch, w_out, ffn2_norm, ffn2_w_gate, ffn2_w_up, ffn2_w_down, final_norm):
    batch, seq, d = x.shape
    depth = w_in.shape[0]
    hk = gla_w_gate_up.shape[2]
    hv = gla_out_norm.shape[1] * gla_out_norm.shape[2]
    bw = d // 2
    rank = gla_w_gate_up.shape[1]
    lr_lo = 2 * hk + 2 * hv

    xf = x.reshape(batch * seq, d)
    memf = mem.reshape(-1, d)
    for l in range(depth):
        last = l == depth - 1
        x1, h, wg2, wu2, wd2 = _ffn(
            xf, ffn1_norm[l][None], ffn1_w_gate[l].astype(BF16), ffn1_w_up[l].astype(BF16),
            ffn1_w_down[l].astype(BF16), post_g=mix_norm[l][None],
            casts=(ffn2_w_gate[l], ffn2_w_up[l], ffn2_w_down[l]), keep_resid=True)

        w_in_t = jnp.swapaxes(w_in[l], 0, 1)
        w_main, w_lr = _regroup(w_in_t, cut_lo=lr_lo, gap=rank)
        proj = _matmul(h, w_main, BF16, tm=1024, tn=2048, name="proj")
        w_up = jnp.pad(gla_w_gate_up[l], ((0, LANES - rank), (0, 0)))
        w_la = _matmul(w_lr, w_up, BF16, tm=d, tn=hk, name="fold_gate", precision=HIGHEST)
        la, wb, wo, wkv = _la_proj(h, w_la, gla_gate_bias[l][None],
                                   casts=(w_branch[l].reshape(-1, d), w_out[l], w_kv_mem[l]))
        kv = _norm_matmul(memf, mem_norm[l][None], wkv, BF16, tm=512, name="mem_kv")

        y_gla = _gla(proj, la, gla_out_norm[l].reshape(1, hv), batch=batch, seq=seq)
        x2 = _merge(y_gla, proj, kv, sg_ln_g[l], sg_ln_b[l], sg_w_s[l], sg_b_s[l][:, :, None],
                    wb.reshape(w_branch[l].shape), wo, x1,
                    u_block=lr_lo // bw, q_block=lr_lo // bw + 2,
                    gate_block=(lr_lo + 3 * bw) // d, seq=seq)

        (xf,) = _ffn(x2, ffn2_norm[l][None], wg2, wu2, wd2,
                     post_g=final_norm[None] if last else None, keep_resid=not last)
    return xf.reshape(batch, seq, d)
```

```python
import functools

import jax
import jax.numpy as jnp
from jax import lax
from jax.experimental import pallas as pl
from jax.experimental.pallas import tpu as pltpu

F32 = jnp.float32
BF16 = jnp.bfloat16
HIGHEST = lax.Precision.HIGHEST
NT_DIMS = (((1,), (1,)), ((), ()))
TN_DIMS = (((0,), (0,)), ((), ()))

EPS = 1e-6
LOG2_E = 1.4426950408889634
GLA_HEADS = 4
GLA_TAU = 16.0
GLA_CHUNK = 128
GLA_SUB = 8
XA_HEADS = 4

ROW_SLAB = 256
LANES = 128
VMEM_LIMIT = 60 * 1024 * 1024


def _params(semantics):
    return pltpu.CompilerParams(dimension_semantics=semantics,
                                vmem_limit_bytes=VMEM_LIMIT)


def _rms(x, g):
    return x * lax.rsqrt(jnp.mean(x * x, axis=-1, keepdims=True) + EPS) * g


def _ffn_body(x_ref, g_ref, wg_ref, wu_ref, wd_ref, *refs, post_norm, keep_resid, n_casts,
              half):
    refs = list(refs)
    pg_ref = refs.pop(0) if post_norm else None
    cast_in = [refs.pop(0) for _ in range(n_casts)]
    o_ref = refs.pop(0)
    emit_h = post_norm and keep_resid
    h_hbm = refs.pop(0) if emit_h else None
    cast_out = [refs.pop(0) for _ in range(n_casts)]
    xn_ref = refs.pop(0)
    if emit_h:
        h_buf, h_sem = refs
    i = pl.program_id(0)
    f = pl.program_id(1)
    tm = x_ref.shape[0]

    def h_copy(tile):
        return pltpu.make_async_copy(h_buf, h_hbm.at[pl.ds(tile * tm, tm), :], h_sem)

    slabs = [slice(r, r + ROW_SLAB) for r in range(0, tm, ROW_SLAB)]

    @pl.when(f == 0)
    def _():
        for rows in slabs:
            x = x_ref[rows, :]
            xn_ref[rows, :] = _rms(x, g_ref[...]).astype(BF16)
            o_ref[rows, :] = x

    for src_ref, dst_ref in zip(cast_in, cast_out):
        dst_ref[...] = src_ref[...].astype(BF16)

    for r in range(0, tm, half):
        rows = slice(r, r + half)
        xn = xn_ref[rows, :]
        gate = jnp.dot(xn, wg_ref[...], preferred_element_type=F32)
        up = jnp.dot(xn, wu_ref[...], preferred_element_type=F32)
        hid = (jax.nn.silu(gate) * up * 0.5).astype(BF16)
        o_ref[rows, :] += jnp.dot(hid, wd_ref[...], preferred_element_type=F32)

    @pl.when(f == pl.num_programs(1) - 1)
    def _():
        if emit_h:
            @pl.when(i > 0)
            def _():
                h_copy(i - 1).wait()

        if post_norm:
            for rows in slabs:
                yn = _rms(o_ref[rows, :], pg_ref[...])
                if emit_h:
                    h_buf[rows, :] = yn.astype(BF16)
                else:
                    o_ref[rows, :] = yn

        if emit_h:
            h_copy(i).start()

            @pl.when(i == pl.num_programs(0) - 1)
            def _():
                h_copy(i).wait()


def _ffn(x, g, wg, wu, wd, post_g=None, casts=(), *, keep_resid, tm=1024, tf=512, half=512):
    assert keep_resid or post_g is not None
    emit_h = keep_resid and post_g is not None
    n, d = x.shape
    dff = wg.shape[1]
    ni, nf = n // tm, dff // tf
    row = lambda i, f: (i, 0)
    vec = pl.BlockSpec((1, d), lambda i, f: (0, 0))
    in_specs = [pl.BlockSpec((tm, d), row), vec,
                pl.BlockSpec((d, tf), lambda i, f: (0, f)),
                pl.BlockSpec((d, tf), lambda i, f: (0, f)),
                pl.BlockSpec((tf, d), lambda i, f: (f, 0))]
    args = [x, g, wg, wu, wd]
    if post_g is not None:
        in_specs.append(vec)
        args.append(post_g)
    cast_specs = []
    for a in casts:
        rws, cls = a.shape
        if cls % nf == 0 and (cls // nf) % LANES == 0 and rws % ni == 0:
            cast_specs.append(pl.BlockSpec((rws // ni, cls // nf), lambda i, f: (i, f)))
        else:
            assert rws % (ni * nf) == 0
            cast_specs.append(pl.BlockSpec((rws // (ni * nf), cls),
                                           lambda i, f: (i * nf + f, 0)))
    out_specs = [pl.BlockSpec((tm, d), row)]
    out_shape = [jax.ShapeDtypeStruct((n, d), F32)]
    scratch = [pltpu.VMEM((tm, d), BF16)]
    if emit_h:
        out_specs.append(pl.BlockSpec(memory_space=pl.ANY))
        out_shape.append(jax.ShapeDtypeStruct((n, d), BF16))
        scratch += [pltpu.VMEM((tm, d), BF16), pltpu.SemaphoreType.DMA(())]
    return pl.pallas_call(
        functools.partial(_ffn_body, post_norm=post_g is not None, keep_resid=keep_resid,
                          n_casts=len(casts), half=half),
        grid=(ni, nf),
        in_specs=in_specs + cast_specs,
        out_specs=out_specs + cast_specs,
        out_shape=out_shape + [jax.ShapeDtypeStruct(a.shape, BF16) for a in casts],
        scratch_shapes=scratch,
        compiler_params=_params(("arbitrary", "arbitrary")),
        name="ffn",
    )(*args, *casts)


def _regroup_body(a_ref, b_ref, o_ref, cut_ref, *, cut_blocks, gap):
    c = pl.program_id(0)

    @pl.when(c < cut_blocks)
    def _():
        o_ref[...] = a_ref[...].T.astype(BF16)

    @pl.when(c == cut_blocks - 1)
    def _():
        pad = jnp.zeros((cut_ref.shape[1] - gap, b_ref.shape[1]), F32)
        cut_ref[...] = jnp.concatenate([b_ref[...], pad], axis=0).T

    @pl.when(c >= cut_blocks)
    def _():
        rows = jnp.concatenate([a_ref[gap:, :], b_ref[...]], axis=0)
        o_ref[...] = rows.T.astype(BF16)


def _regroup(wt, *, cut_lo, gap, tc=512):
    width, d = wt.shape
    out_w = width - gap
    assert cut_lo % tc == 0 and cut_lo >= tc and out_w % tc == 0 and tc % gap == 0
    return pl.pallas_call(
        functools.partial(_regroup_body, cut_blocks=cut_lo // tc, gap=gap),
        grid=(out_w // tc,),
        in_specs=[pl.BlockSpec((tc, d), lambda c: (c, 0)),
                  pl.BlockSpec((gap, d), lambda c: ((c + 1) * (tc // gap), 0))],
        out_specs=[pl.BlockSpec((d, tc), lambda c: (0, c)),
                   pl.BlockSpec((d, LANES), lambda c: (0, 0))],
        out_shape=[jax.ShapeDtypeStruct((d, out_w), BF16),
                   jax.ShapeDtypeStruct((d, LANES), F32)],
        compiler_params=_params(("arbitrary",)),
        name="regroup",
    )(wt, wt)


def _matmul_body(x_ref, w_ref, o_ref, *, precision):
    o_ref[...] = jnp.dot(x_ref[...], w_ref[...], preferred_element_type=F32,
                         precision=precision).astype(o_ref.dtype)


def _matmul(x, w, out_dtype, *, tm, tn, name, precision=None):
    n, d = x.shape
    nc = w.shape[1]
    return pl.pallas_call(
        functools.partial(_matmul_body, precision=precision),
        grid=(n // tm, nc // tn),
        in_specs=[pl.BlockSpec((tm, d), lambda i, j: (i, 0)),
                  pl.BlockSpec((d, tn), lambda i, j: (0, j))],
        out_specs=pl.BlockSpec((tm, tn), lambda i, j: (i, j)),
        out_shape=jax.ShapeDtypeStruct((n, nc), out_dtype),
        compiler_params=_params(("parallel", "parallel")),
        name=name,
    )(x, w)


def _norm_matmul_body(x_ref, g_ref, w_ref, o_ref):
    xn = _rms(x_ref[...], g_ref[...]).astype(BF16)
    o_ref[...] = jnp.dot(xn, w_ref[...], preferred_element_type=F32).astype(o_ref.dtype)


def _norm_matmul(x, g, w, out_dtype, *, tm, name):
    n, d = x.shape
    nc = w.shape[1]
    return pl.pallas_call(
        _norm_matmul_body,
        grid=(n // tm,),
        in_specs=[pl.BlockSpec((tm, d), lambda i: (i, 0)),
                  pl.BlockSpec((1, d), lambda i: (0, 0)),
                  pl.BlockSpec((d, nc), lambda i: (0, 0))],
        out_specs=pl.BlockSpec((tm, nc), lambda i: (i, 0)),
        out_shape=jax.ShapeDtypeStruct((n, nc), out_dtype),
        compiler_params=_params(("parallel",)),
        name=name,
    )(x, g, w)


def _la_body(x_ref, w_ref, b_ref, o_ref):
    z = jnp.dot(x_ref[...], w_ref[...], preferred_element_type=F32) + b_ref[...]
    o_ref[...] = jax.nn.log_sigmoid(z) * (1.0 / GLA_TAU)


def _la_proj(x, w, bias, *, tm=1024):
    n, d = x.shape
    nc = w.shape[1]
    return pl.pallas_call(
        _la_body,
        grid=(n // tm,),
        in_specs=[pl.BlockSpec((tm, d), lambda i: (i, 0)),
                  pl.BlockSpec((d, nc), lambda i: (0, 0)),
                  pl.BlockSpec((1, nc), lambda i: (0, 0))],
        out_specs=pl.BlockSpec((tm, nc), lambda i: (i, 0)),
        out_shape=jax.ShapeDtypeStruct((n, nc), F32),
        compiler_params=_params(("parallel",)),
        name="la_proj",
    )(x, w, bias)


def _gla_body(q_ref, k_ref, v_ref, r_ref, la_ref, on_ref, *refs, heads, chunk, sub, n_casts):
    cast_in, y_ref, cast_out = refs[:n_casts], refs[n_casts], refs[n_casts + 1:2 * n_casts + 1]
    st_ref, kb_ref = refs[2 * n_casts + 1:]
    for src_ref, dst_ref in zip(cast_in, cast_out):
        dst_ref[...] = src_ref[...].astype(BF16)
    ts = q_ref.shape[0]
    dk = q_ref.shape[1] // heads
    dv = v_ref.shape[1] // heads
    scale = dk ** -0.5

    @pl.when(pl.program_id(1) == 0)
    def _():
        st_ref[...] = jnp.zeros_like(st_ref)

    t_idx = lax.broadcasted_iota(jnp.int32, (chunk, chunk), 0)
    s_idx = lax.broadcasted_iota(jnp.int32, (chunk, chunk), 1)
    causal = t_idx >= s_idx
    ones_tril = causal.astype(BF16)
    lane_in_sub = lax.broadcasted_iota(jnp.int32, (sub, chunk), 1) % sub

    halves = []
    m = chunk // 2
    while m >= sub:
        halves.append(m)
        m //= 2
    lane_half = {m: lax.broadcasted_iota(jnp.int32, (m, chunk), 1) >> (m.bit_length() - 1)
                 for m in halves}

    for c in range(ts // chunk):
        rows = slice(c * chunk, (c + 1) * chunk)
        kb = kb_ref.at[c % 2]
        la = la_ref[rows, :] * LOG2_E
        la_hi = la.astype(BF16)
        la_lo = (la - la_hi.astype(F32)).astype(BF16)
        b_all = (jnp.dot(ones_tril, la_hi, preferred_element_type=F32)
                 + jnp.dot(ones_tril, la_lo, preferred_element_type=F32))
        for h in range(heads):
            kcols = slice(h * dk, (h + 1) * dk)
            vcols = slice(h * dv, (h + 1) * dv)
            b = b_all[:, kcols]
            q = q_ref[rows, kcols].astype(F32) * scale
            k = k_ref[rows, kcols].astype(F32)
            v = v_ref[rows, vcols]
            st = st_ref[h]
            inter = lax.dot_general((q * jnp.exp2(b)).astype(BF16), st.astype(BF16),
                                    NT_DIMS, preferred_element_type=F32)

            kb[h, 0] = k
            kb[h, 1] = b
            diag = []
            for i in range(chunk // sub):
                lo = i * sub
                qi, bi = q[lo:lo + sub], b[lo:lo + sub]
                d = jnp.zeros((sub, chunk), F32)
                for s in range(sub):
                    ks = jnp.broadcast_to(kb[h, 0, lo + s:lo + s + 1, :], (sub, dk))
                    bs = jnp.broadcast_to(kb[h, 1, lo + s:lo + s + 1, :], (sub, dk))
                    col = jnp.sum(qi * ks * jnp.exp2(bi - bs), axis=-1, keepdims=True)
                    d = jnp.where(lane_in_sub == s, col, d)
                diag.append(d)
            scores = jnp.concatenate(diag, axis=0)

            for m in halves:
                parts = []
                for j in range(0, chunk, 2 * m):
                    b_mid = b[j + m - 1:j + m]
                    parts.append(k[j:j + m] * jnp.exp2(b_mid - b[j:j + m]))
                    parts.append(q[j + m:j + 2 * m] * jnp.exp2(b[j + m:j + 2 * m] - b_mid))
                z = jnp.concatenate(parts, axis=0).astype(BF16)
                pair = lax.dot_general(z, z, NT_DIMS, preferred_element_type=F32)
                out = []
                for j in range(0, chunk, 2 * m):
                    out.append(scores[j:j + m])
                    out.append(jnp.where(lane_half[m] == j // m, pair[j + m:j + 2 * m],
                                         scores[j + m:j + 2 * m]))
                scores = jnp.concatenate(out, axis=0)
            scores = jnp.where(causal, scores, 0.0)
            o = inter + jnp.dot(scores.astype(BF16), v, preferred_element_type=F32)

            b_end = b[chunk - 1:chunk]
            k_end = (k * jnp.exp2(b_end - b)).astype(BF16)
            upd = lax.dot_general(v, k_end, TN_DIMS, preferred_element_type=F32)
            st_ref[h] = st * jnp.exp2(b_end) + upd

            o = o * lax.rsqrt(jnp.mean(o * o, axis=-1, keepdims=True) + EPS)
            o = o * on_ref[:, vcols] * jax.nn.silu(r_ref[rows, vcols].astype(F32))
            y_ref[rows, vcols] = o.astype(y_ref.dtype)


def _gla(proj, la, onorm, casts=(), *, batch, seq, ts=1024):
    hk = la.shape[1]
    hv = onorm.shape[1]
    ns = seq // ts
    row = lambda b, s: b * ns + s
    steps = batch * ns
    assert all(a.shape[0] % steps == 0 for a in casts)
    cast_specs = [pl.BlockSpec((a.shape[0] // steps, a.shape[1]), lambda b, s: (row(b, s), 0))
                  for a in casts]
    return pl.pallas_call(
        functools.partial(_gla_body, heads=GLA_HEADS, chunk=GLA_CHUNK, sub=GLA_SUB,
                          n_casts=len(casts)),
        grid=(batch, ns),
        in_specs=[
            pl.BlockSpec((ts, hk), lambda b, s: (row(b, s), 0)),
            pl.BlockSpec((ts, hk), lambda b, s: (row(b, s), 1)),
            pl.BlockSpec((ts, hv), lambda b, s: (row(b, s), hk * 2 // hv)),
            pl.BlockSpec((ts, hv), lambda b, s: (row(b, s), hk * 2 // hv + 1)),
            pl.BlockSpec((ts, hk), lambda b, s: (row(b, s), 0)),
            pl.BlockSpec((1, hv), lambda b, s: (0, 0)),
        ] + cast_specs,
        out_specs=[pl.BlockSpec((ts, hv), lambda b, s: (row(b, s), 0))] + cast_specs,
        out_shape=[jax.ShapeDtypeStruct((batch * seq, hv), BF16)]
                  + [jax.ShapeDtypeStruct(a.shape, BF16) for a in casts],
        scratch_shapes=[pltpu.VMEM((GLA_HEADS, hv // GLA_HEADS, hk // GLA_HEADS), F32),
                        pltpu.VMEM((2, GLA_HEADS, 2, GLA_CHUNK, hk // GLA_HEADS), F32)],
        compiler_params=_params(("parallel", "arbitrary")),
        name="gla",
    )(proj, proj, proj, proj, la, onorm, *casts)


def _gelu(x):
    return 0.5 * x * (1.0 + lax.erf(x * (2.0 ** -0.5)))


def _sg_body(u_ref, v_ref, lg_ref, lb_ref, ws_ref, bs_ref, y_ref, *, groups, chunk):
    ts = u_ref.shape[0]
    gdim = u_ref.shape[1] // groups
    rr = lax.broadcasted_iota(jnp.int32, (chunk, chunk), 0)
    cc = lax.broadcasted_iota(jnp.int32, (chunk, chunk), 1)
    for g in range(groups):
        cols = slice(g * gdim, (g + 1) * gdim)
        w = jnp.where(rr >= cc, ws_ref[g], 0.0).astype(BF16)
        for c in range(ts // chunk):
            rows = slice(c * chunk, (c + 1) * chunk)
            u = _gelu(u_ref[rows, cols].astype(F32))
            v = _gelu(v_ref[rows, cols].astype(F32))
            mu = jnp.mean(v, axis=-1, keepdims=True)
            var = jnp.mean((v - mu) ** 2, axis=-1, keepdims=True)
            vn = (v - mu) * lax.rsqrt(var + EPS) * lg_ref[g:g + 1, :] + lb_ref[g:g + 1, :]
            vs = jnp.dot(w, vn.astype(BF16), preferred_element_type=F32) + bs_ref[g]
            y_ref[rows, cols] = (u * vs).astype(y_ref.dtype)


def _xa_body(q_ref, k_ref, v_ref, y_ref, *, heads):
    dh = q_ref.shape[1] // heads
    for h in range(heads):
        cols = slice(h * dh, (h + 1) * dh)
        s = lax.dot_general(q_ref[:, cols], k_ref[:, cols], NT_DIMS,
                            preferred_element_type=F32) * (dh ** -0.5)
        p = jnp.exp(s - jnp.max(s, axis=-1, keepdims=True))
        p = p / jnp.sum(p, axis=-1, keepdims=True)
        y_ref[:, cols] = jnp.dot(p.astype(BF16), v_ref[:, cols],
                                 preferred_element_type=F32).astype(y_ref.dtype)


def _merge_body(su0_ref, svq0_ref, kv0_ref, su_ref, svq_ref, kv_ref,
                ya_ref, g_ref, lg_ref, lb_ref, ws_ref, bs_ref, wb_ref, wo_ref, x_ref, o_ref,
                yb_cur, yc_cur, yb_nxt, yc_nxt, *, groups, chunk, heads):
    bw = su_ref.shape[1]
    d = x_ref.shape[1]
    lo, hi = pl.ds(0, bw), pl.ds(bw, bw)

    def branches(u_ref, vq_ref, mem_ref, yb_ref, yc_ref):
        _sg_body(u_ref, vq_ref.at[:, lo], lg_ref, lb_ref, ws_ref, bs_ref, yb_ref,
                 groups=groups, chunk=chunk)
        _xa_body(vq_ref.at[:, hi], mem_ref.at[:, lo], mem_ref.at[:, hi], yc_ref, heads=heads)

    @pl.when(pl.program_id(0) == 0)
    def _():
        branches(su0_ref, svq0_ref, kv0_ref, yb_cur, yc_cur)

    merged = None
    for i, y_ref in enumerate((ya_ref, yb_cur, yc_cur)):
        t = jnp.dot(y_ref[...], wb_ref[i], preferred_element_type=F32)
        t = jax.nn.sigmoid(g_ref[:, i * d:(i + 1) * d].astype(F32)) * t
        merged = t if merged is None else merged + t
    o_ref[...] = x_ref[...] + jnp.dot(merged.astype(BF16), wo_ref[...],
                                      preferred_element_type=F32)

    branches(su_ref, svq_ref, kv_ref, yb_nxt, yc_nxt)
    yb_cur[...] = yb_nxt[...]
    yc_cur[...] = yc_nxt[...]


def _merge(ya, proj, kv, ln_g, ln_b, w_s, b_s, wb, wo, x, *, u_block, gate_col, seq, tm=256):
    n, d = x.shape
    bw = ya.shape[1]
    groups, gdim = ln_g.shape
    chunk = w_s.shape[1]
    mem_len = kv.shape[0] // (n // seq)
    tiles_per_seq = seq // tm
    last = n // tm - 1
    assert tm % chunk == 0 and groups * gdim == bw and kv.shape[1] == 2 * bw
    assert (u_block + 1) % 2 == 0 and gate_col % (3 * d) == 0
    svq_block = (u_block + 1) // 2
    row = lambda i: (i, 0)
    nxt = lambda i: jnp.minimum(i + 1, last)
    const2 = lambda i: (0, 0)
    const3 = lambda i: (0, 0, 0)
    once = pl.Buffered(1)
    return pl.pallas_call(
        functools.partial(_merge_body, groups=groups, chunk=chunk, heads=XA_HEADS),
        grid=(n // tm,),
        in_specs=[
            pl.BlockSpec((tm, bw), lambda i: (0, u_block)),
            pl.BlockSpec((tm, 2 * bw), lambda i: (0, svq_block)),
            pl.BlockSpec((mem_len, 2 * bw), const2),
            pl.BlockSpec((tm, bw), lambda i: (nxt(i), u_block)),
            pl.BlockSpec((tm, 2 * bw), lambda i: (nxt(i), svq_block)),
            pl.BlockSpec((mem_len, 2 * bw), lambda i: (nxt(i) // tiles_per_seq, 0)),
            pl.BlockSpec((tm, bw), row),
            pl.BlockSpec((tm, 3 * d), lambda i: (i, gate_col // (3 * d))),
            pl.BlockSpec((groups, gdim), const2),
            pl.BlockSpec((groups, gdim), const2),
            pl.BlockSpec((groups, chunk, chunk), const3),
            pl.BlockSpec((groups, chunk, 1), const3),
            pl.BlockSpec(wb.shape, const3, pipeline_mode=once),
            pl.BlockSpec(wo.shape, const2, pipeline_mode=once),
            pl.BlockSpec((tm, d), row),
        ],
        out_specs=pl.BlockSpec((tm, d), row),
        out_shape=jax.ShapeDtypeStruct((n, d), F32),
        scratch_shapes=[pltpu.VMEM((tm, bw), BF16)] * 4,
        compiler_params=_params(("arbitrary",)),
        name="merge",
    )(proj, proj, kv, proj, proj, kv, ya, proj, ln_g, ln_b, w_s, b_s, wb, wo, x)


def kernel(x, mem, ffn1_norm, ffn1_w_gate, ffn1_w_up, ffn1_w_down, mix_norm, mem_norm, w_in, gla_w_gate_up, gla_gate_bias, gla_out_norm, sg_ln_g, sg_ln_b, sg_w_s, sg_b_s, w_kv_mem, w_branch, w_out, ffn2_norm, ffn2_w_gate, ffn2_w_up, ffn2_w_down, final_norm):
    batch, seq, d = x.shape
    depth = w_in.shape[0]
    hk = gla_w_gate_up.shape[2]
    hv = gla_out_norm.shape[1] * gla_out_norm.shape[2]
    bw = d // 2
    rank = gla_w_gate_up.shape[1]
    lr_lo = 2 * hk + 2 * hv

    xf = x.reshape(batch * seq, d)
    memf = mem.reshape(-1, d)
    for l in range(depth):
        last = l == depth - 1
        x1, h, wg2, wu2, wd2 = _ffn(
            xf, ffn1_norm[l][None], ffn1_w_gate[l].astype(BF16), ffn1_w_up[l].astype(BF16),
            ffn1_w_down[l].astype(BF16), post_g=mix_norm[l][None],
            casts=(ffn2_w_gate[l], ffn2_w_up[l], ffn2_w_down[l]), keep_resid=True)

        w_in_t = jnp.swapaxes(w_in[l], 0, 1)
        w_main, w_lr = _regroup(w_in_t, cut_lo=lr_lo, gap=rank)
        proj = _matmul(h, w_main, BF16, tm=1024, tn=2048, name="proj")
        w_up = jnp.pad(gla_w_gate_up[l], ((0, LANES - rank), (0, 0)))
        w_la = _matmul(w_lr, w_up, BF16, tm=d, tn=hk, name="fold_gate", precision=HIGHEST)
        la = _la_proj(h, w_la, gla_gate_bias[l][None])
        y_gla, wb, wo, wkv = _gla(
            proj, la, gla_out_norm[l].reshape(1, hv),
            casts=(w_branch[l].reshape(-1, d), w_out[l], w_kv_mem[l]), batch=batch, seq=seq)
        kv = _norm_matmul(memf, mem_norm[l][None], wkv, BF16, tm=512, name="mem_kv")
        x2 = _merge(y_gla, proj, kv, sg_ln_g[l], sg_ln_b[l], sg_w_s[l], sg_b_s[l][:, :, None],
                    wb.reshape(w_branch[l].shape), wo, x1,
                    u_block=lr_lo // bw, gate_col=lr_lo + 3 * bw, seq=seq)

        (xf,) = _ffn(x2, ffn2_norm[l][None], wg2, wu2, wd2,
                     post_g=final_norm[None] if last else None, keep_resid=not last)
    return xf.reshape(batch, seq, d)
```

```python
import functools

import jax
import jax.numpy as jnp
from jax import lax
from jax.experimental import pallas as pl
from jax.experimental.pallas import tpu as pltpu

F32 = jnp.float32
BF16 = jnp.bfloat16
HIGHEST = lax.Precision.HIGHEST
NT_DIMS = (((1,), (1,)), ((), ()))
TN_DIMS = (((0,), (0,)), ((), ()))

EPS = 1e-6
LOG2_E = 1.4426950408889634
GLA_HEADS = 4
GLA_TAU = 16.0
GLA_CHUNK = 256
GLA_SUB = 8
XA_HEADS = 4

ROW_SLAB = 256
LANES = 128
VMEM_LIMIT = 60 * 1024 * 1024


def _params(semantics):
    return pltpu.CompilerParams(dimension_semantics=semantics,
                                vmem_limit_bytes=VMEM_LIMIT)


def _rms(x, g):
    return x * lax.rsqrt(jnp.mean(x * x, axis=-1, keepdims=True) + EPS) * g


def _ffn_body(x_ref, g_ref, wg_ref, wu_ref, wd_ref, *refs, post_norm, keep_resid, n_casts,
              half):
    refs = list(refs)
    pg_ref = refs.pop(0) if post_norm else None
    cast_in = [refs.pop(0) for _ in range(n_casts)]
    o_ref = refs.pop(0)
    emit_h = post_norm and keep_resid
    h_hbm = refs.pop(0) if emit_h else None
    cast_out = [refs.pop(0) for _ in range(n_casts)]
    xn_ref = refs.pop(0)
    if emit_h:
        h_buf, h_sem = refs
    i = pl.program_id(0)
    f = pl.program_id(1)
    tm = x_ref.shape[0]

    def h_copy(tile):
        return pltpu.make_async_copy(h_buf, h_hbm.at[pl.ds(tile * tm, tm), :], h_sem)

    slabs = [slice(r, r + ROW_SLAB) for r in range(0, tm, ROW_SLAB)]

    @pl.when(f == 0)
    def _():
        for rows in slabs:
            x = x_ref[rows, :]
            xn_ref[rows, :] = _rms(x, g_ref[...]).astype(BF16)
            o_ref[rows, :] = x

    for src_ref, dst_ref in zip(cast_in, cast_out):
        dst_ref[...] = src_ref[...].astype(BF16)

    for r in range(0, tm, half):
        rows = slice(r, r + half)
        xn = xn_ref[rows, :]
        gate = jnp.dot(xn, wg_ref[...], preferred_element_type=F32)
        up = jnp.dot(xn, wu_ref[...], preferred_element_type=F32)
        hid = (jax.nn.silu(gate) * up * 0.5).astype(BF16)
        o_ref[rows, :] += jnp.dot(hid, wd_ref[...], preferred_element_type=F32)

    @pl.when(f == pl.num_programs(1) - 1)
    def _():
        if emit_h:
            @pl.when(i > 0)
            def _():
                h_copy(i - 1).wait()

        if post_norm:
            for rows in slabs:
                yn = _rms(o_ref[rows, :], pg_ref[...])
                if emit_h:
                    h_buf[rows, :] = yn.astype(BF16)
                else:
                    o_ref[rows, :] = yn

        if emit_h:
            h_copy(i).start()

            @pl.when(i == pl.num_programs(0) - 1)
            def _():
                h_copy(i).wait()


def _ffn(x, g, wg, wu, wd, post_g=None, casts=(), *, keep_resid, tm=1024, tf=512, half=512):
    assert keep_resid or post_g is not None
    emit_h = keep_resid and post_g is not None
    n, d = x.shape
    dff = wg.shape[1]
    ni, nf = n // tm, dff // tf
    row = lambda i, f: (i, 0)
    vec = pl.BlockSpec((1, d), lambda i, f: (0, 0))
    in_specs = [pl.BlockSpec((tm, d), row), vec,
                pl.BlockSpec((d, tf), lambda i, f: (0, f)),
                pl.BlockSpec((d, tf), lambda i, f: (0, f)),
                pl.BlockSpec((tf, d), lambda i, f: (f, 0))]
    args = [x, g, wg, wu, wd]
    if post_g is not None:
        in_specs.append(vec)
        args.append(post_g)
    cast_specs = []
    for a in casts:
        rws, cls = a.shape
        if cls % nf == 0 and (cls // nf) % LANES == 0 and rws % ni == 0:
            cast_specs.append(pl.BlockSpec((rws // ni, cls // nf), lambda i, f: (i, f)))
        else:
            assert rws % (ni * nf) == 0
            cast_specs.append(pl.BlockSpec((rws // (ni * nf), cls),
                                           lambda i, f: (i * nf + f, 0)))
    out_specs = [pl.BlockSpec((tm, d), row)]
    out_shape = [jax.ShapeDtypeStruct((n, d), F32)]
    scratch = [pltpu.VMEM((tm, d), BF16)]
    if emit_h:
        out_specs.append(pl.BlockSpec(memory_space=pl.ANY))
        out_shape.append(jax.ShapeDtypeStruct((n, d), BF16))
        scratch += [pltpu.VMEM((tm, d), BF16), pltpu.SemaphoreType.DMA(())]
    return pl.pallas_call(
        functools.partial(_ffn_body, post_norm=post_g is not None, keep_resid=keep_resid,
                          n_casts=len(casts), half=half),
        grid=(ni, nf),
        in_specs=in_specs + cast_specs,
        out_specs=out_specs + cast_specs,
        out_shape=out_shape + [jax.ShapeDtypeStruct(a.shape, BF16) for a in casts],
        scratch_shapes=scratch,
        compiler_params=_params(("arbitrary", "arbitrary")),
        name="ffn",
    )(*args, *casts)


def _regroup_body(a_ref, b_ref, o_ref, cut_ref, *, cut_blocks, gap):
    c = pl.program_id(0)

    @pl.when(c < cut_blocks)
    def _():
        o_ref[...] = a_ref[...].T.astype(BF16)

    @pl.when(c == cut_blocks - 1)
    def _():
        pad = jnp.zeros((cut_ref.shape[1] - gap, b_ref.shape[1]), F32)
        cut_ref[...] = jnp.concatenate([b_ref[...], pad], axis=0).T

    @pl.when(c >= cut_blocks)
    def _():
        rows = jnp.concatenate([a_ref[gap:, :], b_ref[...]], axis=0)
        o_ref[...] = rows.T.astype(BF16)


def _regroup(wt, *, cut_lo, gap, tc=512):
    width, d = wt.shape
    out_w = width - gap
    assert cut_lo % tc == 0 and cut_lo >= tc and out_w % tc == 0 and tc % gap == 0
    return pl.pallas_call(
        functools.partial(_regroup_body, cut_blocks=cut_lo // tc, gap=gap),
        grid=(out_w // tc,),
        in_specs=[pl.BlockSpec((tc, d), lambda c: (c, 0)),
                  pl.BlockSpec((gap, d), lambda c: ((c + 1) * (tc // gap), 0))],
        out_specs=[pl.BlockSpec((d, tc), lambda c: (0, c)),
                   pl.BlockSpec((d, LANES), lambda c: (0, 0))],
        out_shape=[jax.ShapeDtypeStruct((d, out_w), BF16),
                   jax.ShapeDtypeStruct((d, LANES), F32)],
        compiler_params=_params(("arbitrary",)),
        name="regroup",
    )(wt, wt)


def _matmul_body(x_ref, w_ref, o_ref, *, precision):
    o_ref[...] = jnp.dot(x_ref[...], w_ref[...], preferred_element_type=F32,
                         precision=precision).astype(o_ref.dtype)


def _matmul(x, w, out_dtype, *, tm, tn, name, precision=None):
    n, d = x.shape
    nc = w.shape[1]
    return pl.pallas_call(
        functools.partial(_matmul_body, precision=precision),
        grid=(n // tm, nc // tn),
        in_specs=[pl.BlockSpec((tm, d), lambda i, j: (i, 0)),
                  pl.BlockSpec((d, tn), lambda i, j: (0, j))],
        out_specs=pl.BlockSpec((tm, tn), lambda i, j: (i, j)),
        out_shape=jax.ShapeDtypeStruct((n, nc), out_dtype),
        compiler_params=_params(("parallel", "parallel")),
        name=name,
    )(x, w)


def _norm_matmul_body(x_ref, g_ref, w_ref, o_ref):
    xn = _rms(x_ref[...], g_ref[...]).astype(BF16)
    o_ref[...] = jnp.dot(xn, w_ref[...], preferred_element_type=F32).astype(o_ref.dtype)


def _norm_matmul(x, g, w, out_dtype, *, tm, name):
    n, d = x.shape
    nc = w.shape[1]
    return pl.pallas_call(
        _norm_matmul_body,
        grid=(n // tm,),
        in_specs=[pl.BlockSpec((tm, d), lambda i: (i, 0)),
                  pl.BlockSpec((1, d), lambda i: (0, 0)),
                  pl.BlockSpec((d, nc), lambda i: (0, 0))],
        out_specs=pl.BlockSpec((tm, nc), lambda i: (i, 0)),
        out_shape=jax.ShapeDtypeStruct((n, nc), out_dtype),
        compiler_params=_params(("parallel",)),
        name=name,
    )(x, g, w)


def _la_body(x_ref, w_ref, b_ref, *refs, n_casts):
    cast_in, o_ref, cast_out = refs[:n_casts], refs[n_casts], refs[n_casts + 1:]
    z = jnp.dot(x_ref[...], w_ref[...], preferred_element_type=F32) + b_ref[...]
    o_ref[...] = jax.nn.log_sigmoid(z) * (1.0 / GLA_TAU)
    for src_ref, dst_ref in zip(cast_in, cast_out):
        dst_ref[...] = src_ref[...].astype(BF16)


def _la_proj(x, w, bias, casts=(), *, tm=1024):
    n, d = x.shape
    nc = w.shape[1]
    ni = n // tm
    assert all(a.shape[0] % ni == 0 for a in casts)
    cast_specs = [pl.BlockSpec((a.shape[0] // ni, a.shape[1]), lambda i: (i, 0))
                  for a in casts]
    return pl.pallas_call(
        functools.partial(_la_body, n_casts=len(casts)),
        grid=(ni,),
        in_specs=[pl.BlockSpec((tm, d), lambda i: (i, 0)),
                  pl.BlockSpec((d, nc), lambda i: (0, 0)),
                  pl.BlockSpec((1, nc), lambda i: (0, 0))] + cast_specs,
        out_specs=[pl.BlockSpec((tm, nc), lambda i: (i, 0))] + cast_specs,
        out_shape=[jax.ShapeDtypeStruct((n, nc), F32)]
                  + [jax.ShapeDtypeStruct(a.shape, BF16) for a in casts],
        compiler_params=_params(("parallel",)),
        name="la_proj",
    )(x, w, bias, *casts)


def _gla_body(q_ref, k_ref, v_ref, r_ref, la_ref, on_ref, y_ref, st_ref, kb_ref,
              *, heads, chunk, sub):
    ts = q_ref.shape[0]
    dk = q_ref.shape[1] // heads
    dv = v_ref.shape[1] // heads
    scale = dk ** -0.5

    @pl.when(pl.program_id(1) == 0)
    def _():
        st_ref[...] = jnp.zeros_like(st_ref)

    t_idx = lax.broadcasted_iota(jnp.int32, (chunk, chunk), 0)
    s_idx = lax.broadcasted_iota(jnp.int32, (chunk, chunk), 1)
    causal = t_idx >= s_idx
    ones_tril = causal.astype(BF16)
    lane_in_sub = lax.broadcasted_iota(jnp.int32, (sub, chunk), 1) % sub

    halves = []
    m = chunk // 2
    while m >= sub:
        halves.append(m)
        m //= 2
    lane_half = {m: lax.broadcasted_iota(jnp.int32, (m, chunk), 1) >> (m.bit_length() - 1)
                 for m in halves}

    for c in range(ts // chunk):
        rows = slice(c * chunk, (c + 1) * chunk)
        kb = kb_ref.at[c % 2]
        la = la_ref[rows, :] * LOG2_E
        la_hi = la.astype(BF16)
        la_lo = (la - la_hi.astype(F32)).astype(BF16)
        b_all = (jnp.dot(ones_tril, la_hi, preferred_element_type=F32)
                 + jnp.dot(ones_tril, la_lo, preferred_element_type=F32))
        for h in range(heads):
            kcols = slice(h * dk, (h + 1) * dk)
            vcols = slice(h * dv, (h + 1) * dv)
            b = b_all[:, kcols]
            q = q_ref[rows, kcols].astype(F32) * scale
            k = k_ref[rows, kcols].astype(F32)
            v = v_ref[rows, vcols]
            st = st_ref[h]
            inter = lax.dot_general((q * jnp.exp2(b)).astype(BF16), st.astype(BF16),
                                    NT_DIMS, preferred_element_type=F32)

            kb[h, 0] = k
            kb[h, 1] = b
            diag = []
            for i in range(chunk // sub):
                lo = i * sub
                qi, bi = q[lo:lo + sub], b[lo:lo + sub]
                d = jnp.zeros((sub, chunk), F32)
                for s in range(sub):
                    ks = jnp.broadcast_to(kb[h, 0, lo + s:lo + s + 1, :], (sub, dk))
                    bs = jnp.broadcast_to(kb[h, 1, lo + s:lo + s + 1, :], (sub, dk))
                    col = jnp.sum(qi * ks * jnp.exp2(bi - bs), axis=-1, keepdims=True)
                    d = jnp.where(lane_in_sub == s, col, d)
                diag.append(d)
            scores = jnp.concatenate(diag, axis=0)

            for m in halves:
                parts = []
                for j in range(0, chunk, 2 * m):
                    b_mid = b[j + m - 1:j + m]
                    parts.append(k[j:j + m] * jnp.exp2(b_mid - b[j:j + m]))
                    parts.append(q[j + m:j + 2 * m] * jnp.exp2(b[j + m:j + 2 * m] - b_mid))
                z = jnp.concatenate(parts, axis=0).astype(BF16)
                pair = lax.dot_general(z, z, NT_DIMS, preferred_element_type=F32)
                out = []
                for j in range(0, chunk, 2 * m):
                    out.append(scores[j:j + m])
                    out.append(jnp.where(lane_half[m] == j // m, pair[j + m:j + 2 * m],
                                         scores[j + m:j + 2 * m]))
                scores = jnp.concatenate(out, axis=0)
            scores = jnp.where(causal, scores, 0.0)
            o = inter + jnp.dot(scores.astype(BF16), v, preferred_element_type=F32)

            b_end = b[chunk - 1:chunk]
            k_end = (k * jnp.exp2(b_end - b)).astype(BF16)
            upd = lax.dot_general(v, k_end, TN_DIMS, preferred_element_type=F32)
            st_ref[h] = st * jnp.exp2(b_end) + upd

            o = o * lax.rsqrt(jnp.mean(o * o, axis=-1, keepdims=True) + EPS)
            o = o * on_ref[:, vcols] * jax.nn.silu(r_ref[rows, vcols].astype(F32))
            y_ref[rows, vcols] = o.astype(y_ref.dtype)


def _gla(proj, la, onorm, *, batch, seq, ts=1024):
    hk = la.shape[1]
    hv = onorm.shape[1]
    ns = seq // ts
    row = lambda b, s: b * ns + s
    return pl.pallas_call(
        functools.partial(_gla_body, heads=GLA_HEADS, chunk=GLA_CHUNK, sub=GLA_SUB),
        grid=(batch, ns),
        in_specs=[
            pl.BlockSpec((ts, hk), lambda b, s: (row(b, s), 0)),
            pl.BlockSpec((ts, hk), lambda b, s: (row(b, s), 1)),
            pl.BlockSpec((ts, hv), lambda b, s: (row(b, s), hk * 2 // hv)),
            pl.BlockSpec((ts, hv), lambda b, s: (row(b, s), hk * 2 // hv + 1)),
            pl.BlockSpec((ts, hk), lambda b, s: (row(b, s), 0)),
            pl.BlockSpec((1, hv), lambda b, s: (0, 0)),
        ],
        out_specs=pl.BlockSpec((ts, hv), lambda b, s: (row(b, s), 0)),
        out_shape=jax.ShapeDtypeStruct((batch * seq, hv), BF16),
        scratch_shapes=[pltpu.VMEM((GLA_HEADS, hv // GLA_HEADS, hk // GLA_HEADS), F32),
                        pltpu.VMEM((2, GLA_HEADS, 2, GLA_CHUNK, hk // GLA_HEADS), F32)],
        compiler_params=_params(("parallel", "arbitrary")),
        name="gla",
    )(proj, proj, proj, proj, la, onorm)


def _gelu(x):
    return 0.5 * x * (1.0 + lax.erf(x * (2.0 ** -0.5)))


def _sg_body(u_ref, v_ref, lg_ref, lb_ref, ws_ref, bs_ref, y_ref, *, groups, chunk):
    ts = u_ref.shape[0]
    gdim = u_ref.shape[1] // groups
    rr = lax.broadcasted_iota(jnp.int32, (chunk, chunk), 0)
    cc = lax.broadcasted_iota(jnp.int32, (chunk, chunk), 1)
    for g in range(groups):
        cols = slice(g * gdim, (g + 1) * gdim)
        w = jnp.where(rr >= cc, ws_ref[g], 0.0).astype(BF16)
        for c in range(ts // chunk):
            rows = slice(c * chunk, (c + 1) * chunk)
            u = _gelu(u_ref[rows, cols].astype(F32))
            v = _gelu(v_ref[rows, cols].astype(F32))
            mu = jnp.mean(v, axis=-1, keepdims=True)
            var = jnp.mean((v - mu) ** 2, axis=-1, keepdims=True)
            vn = (v - mu) * lax.rsqrt(var + EPS) * lg_ref[g:g + 1, :] + lb_ref[g:g + 1, :]
            vs = jnp.dot(w, vn.astype(BF16), preferred_element_type=F32) + bs_ref[g]
            y_ref[rows, cols] = (u * vs).astype(y_ref.dtype)


def _xa_body(q_ref, k_ref, v_ref, y_ref, *, heads):
    dh = q_ref.shape[1] // heads
    for h in range(heads):
        cols = slice(h * dh, (h + 1) * dh)
        s = lax.dot_general(q_ref[:, cols], k_ref[:, cols], NT_DIMS,
                            preferred_element_type=F32) * (dh ** -0.5)
        p = jnp.exp(s - jnp.max(s, axis=-1, keepdims=True))
        p = p / jnp.sum(p, axis=-1, keepdims=True)
        y_ref[:, cols] = jnp.dot(p.astype(BF16), v_ref[:, cols],
                                 preferred_element_type=F32).astype(y_ref.dtype)


def _merge_body(su0_ref, sv0_ref, xq0_ref, mk0_ref, mv0_ref,
                su_ref, sv_ref, xq_ref, mk_ref, mv_ref,
                ya_ref, ga_ref, gb_ref, gc_ref, lg_ref, lb_ref, ws_ref, bs_ref,
                wb_ref, wo_ref, x_ref, o_ref,
                yb_cur, yc_cur, yb_nxt, yc_nxt, *, groups, chunk, heads):
    def branches(u_ref, v_ref, q_ref, k_ref, w_ref, yb_ref, yc_ref):
        _sg_body(u_ref, v_ref, lg_ref, lb_ref, ws_ref, bs_ref, yb_ref, groups=groups, chunk=chunk)
        _xa_body(q_ref, k_ref, w_ref, yc_ref, heads=heads)

    @pl.when(pl.program_id(0) == 0)
    def _():
        branches(su0_ref, sv0_ref, xq0_ref, mk0_ref, mv0_ref, yb_cur, yc_cur)

    merged = None
    for i, (y_ref, g_ref) in enumerate(((ya_ref, ga_ref), (yb_cur, gb_ref), (yc_cur, gc_ref))):
        t = jnp.dot(y_ref[...], wb_ref[i], preferred_element_type=F32)
        t = jax.nn.sigmoid(g_ref[...].astype(F32)) * t
        merged = t if merged is None else merged + t
    o_ref[...] = x_ref[...] + jnp.dot(merged.astype(BF16), wo_ref[...],
                                      preferred_element_type=F32)

    branches(su_ref, sv_ref, xq_ref, mk_ref, mv_ref, yb_nxt, yc_nxt)
    yb_cur[...] = yb_nxt[...]
    yc_cur[...] = yc_nxt[...]


def _merge(ya, proj, kv, ln_g, ln_b, w_s, b_s, wb, wo, x, *, u_block, q_block, gate_block,
           seq, tm=256):
    n, d = x.shape
    bw = ya.shape[1]
    groups, gdim = ln_g.shape
    chunk = w_s.shape[1]
    mem_len = kv.shape[0] // (n // seq)
    tiles_per_seq = seq // tm
    last = n // tm - 1
    assert tm % chunk == 0 and groups * gdim == bw and kv.shape[1] == 2 * bw
    row = lambda i: (i, 0)
    nxt = lambda i: jnp.minimum(i + 1, last)
    const2 = lambda i: (0, 0)
    const3 = lambda i: (0, 0, 0)
    once = pl.Buffered(1)
    tile = lambda f: pl.BlockSpec((tm, bw), f)
    mem = lambda f: pl.BlockSpec((mem_len, bw), f)
    return pl.pallas_call(
        functools.partial(_merge_body, groups=groups, chunk=chunk, heads=XA_HEADS),
        grid=(n // tm,),
        in_specs=[
            tile(lambda i: (0, u_block)), tile(lambda i: (0, u_block + 1)),
            tile(lambda i: (0, q_block)),
            mem(lambda i: (0, 0)), mem(lambda i: (0, 1)),
            tile(lambda i: (nxt(i), u_block)), tile(lambda i: (nxt(i), u_block + 1)),
            tile(lambda i: (nxt(i), q_block)),
            mem(lambda i: (nxt(i) // tiles_per_seq, 0)), mem(lambda i: (nxt(i) // tiles_per_seq, 1)),
            tile(row),
            pl.BlockSpec((tm, d), lambda i: (i, gate_block)),
            pl.BlockSpec((tm, d), lambda i: (i, gate_block + 1)),
            pl.BlockSpec((tm, d), lambda i: (i, gate_block + 2)),
            pl.BlockSpec((groups, gdim), const2),
            pl.BlockSpec((groups, gdim), const2),
            pl.BlockSpec((groups, chunk, chunk), const3),
            pl.BlockSpec((groups, chunk, 1), const3),
            pl.BlockSpec(wb.shape, const3, pipeline_mode=once),
            pl.BlockSpec(wo.shape, const2, pipeline_mode=once),
            pl.BlockSpec((tm, d), row),
        ],
        out_specs=pl.BlockSpec((tm, d), row),
        out_shape=jax.ShapeDtypeStruct((n, d), F32),
        scratch_shapes=[pltpu.VMEM((tm, bw), BF16)] * 4,
        compiler_params=_params(("arbitrary",)),
        name="merge",
    )(proj, proj, proj, kv, kv, proj, proj, proj, kv, kv, ya, proj, proj, proj,
      ln_g, ln_b, w_s, b_s, wb, wo, x)


def kernel(x, mem, ffn1_norm, ffn1_w_gate, ffn1_w_up, ffn1_w_down, mix_norm, mem_norm, w_in, gla_w_gate_up, gla_gate_bias, gla_out_norm, sg_ln_g, sg_ln_b, sg_w_s, sg_b_s, w_kv_mem, w_branch, w_out, ffn2_norm, ffn2_w_gate, ffn2_w_up, ffn2_w_down, final_norm):
    batch, seq, d = x.shape
    depth = w_in.shape[0]
    hk = gla_w_gate_up.shape[2]
    hv = gla_out_norm.shape[1] * gla_out_norm.shape[2]
    bw = d // 2
    rank = gla_w_gate_up.shape[1]
    lr_lo = 2 * hk + 2 * hv

    xf = x.reshape(batch * seq, d)
    memf = mem.reshape(-1, d)
    for l in range(depth):
        last = l == depth - 1
        x1, h, wg2, wu2, wd2 = _ffn(
            xf, ffn1_norm[l][None], ffn1_w_gate[l].astype(BF16), ffn1_w_up[l].astype(BF16),
            ffn1_w_down[l].astype(BF16), post_g=mix_norm[l][None],
            casts=(ffn2_w_gate[l], ffn2_w_up[l], ffn2_w_down[l]), keep_resid=True)

        w_in_t = jnp.swapaxes(w_in[l], 0, 1)
        w_main, w_lr = _regroup(w_in_t, cut_lo=lr_lo, gap=rank)
        proj = _matmul(h, w_main, BF16, tm=1024, tn=2048, name="proj")
        w_up = jnp.pad(gla_w_gate_up[l], ((0, LANES - rank), (0, 0)))
        w_la = _matmul(w_lr, w_up, BF16, tm=d, tn=hk, name="fold_gate", precision=HIGHEST)
        la, wb, wo, wkv = _la_proj(h, w_la, gla_gate_bias[l][None],
                                   casts=(w_branch[l].reshape(-1, d), w_out[l], w_kv_mem[l]))
        kv = _norm_matmul(memf, mem_norm[l][None], wkv, BF16, tm=512, name="mem_kv")

        y_gla = _gla(proj, la, gla_out_norm[l].reshape(1, hv), batch=batch, seq=seq)
        x2 = _merge(y_gla, proj, kv, sg_ln_g[l], sg_ln_b[l], sg_w_s[l], sg_b_s[l][:, :, None],
                    wb.reshape(w_branch[l].shape), wo, x1,
                    u_block=lr_lo // bw, q_block=lr_lo // bw + 2,
                    gate_block=(lr_lo + 3 * bw) // d, seq=seq)

        (xf,) = _ffn(x2, ffn2_norm[l][None], wg2, wu2, wd2,
                     post_g=final_norm[None] if last else None, keep_resid=not last)
    return xf.reshape(batch, seq, d)
```

```python
import functools

import jax
import jax.numpy as jnp
from jax import lax
from jax.experimental import pallas as pl
from jax.experimental.pallas import tpu as pltpu

F32 = jnp.float32
BF16 = jnp.bfloat16
HIGHEST = lax.Precision.HIGHEST
NT_DIMS = (((1,), (1,)), ((), ()))
TN_DIMS = (((0,), (0,)), ((), ()))

EPS = 1e-6
LOG2_E = 1.4426950408889634
GLA_HEADS = 4
GLA_TAU = 16.0
GLA_CHUNK = 256
GLA_SUB = 8
XA_HEADS = 4

ROW_SLAB = 256
LANES = 128
VMEM_LIMIT = 60 * 1024 * 1024


def _params(semantics):
    return pltpu.CompilerParams(dimension_semantics=semantics,
                                vmem_limit_bytes=VMEM_LIMIT)


def _rms(x, g):
    return x * lax.rsqrt(jnp.mean(x * x, axis=-1, keepdims=True) + EPS) * g


def _ffn_body(x_ref, g_ref, wg_ref, wu_ref, wd_ref, *refs, post_norm, keep_resid, n_casts,
              half):
    refs = list(refs)
    pg_ref = refs.pop(0) if post_norm else None
    cast_in = [refs.pop(0) for _ in range(n_casts)]
    o_ref = refs.pop(0)
    emit_h = post_norm and keep_resid
    h_hbm = refs.pop(0) if emit_h else None
    cast_out = [refs.pop(0) for _ in range(n_casts)]
    xn_ref = refs.pop(0)
    if emit_h:
        h_buf, h_sem = refs
    i = pl.program_id(0)
    f = pl.program_id(1)
    tm = x_ref.shape[0]

    def h_copy(tile):
        return pltpu.make_async_copy(h_buf, h_hbm.at[pl.ds(tile * tm, tm), :], h_sem)

    slabs = [slice(r, r + ROW_SLAB) for r in range(0, tm, ROW_SLAB)]

    @pl.when(f == 0)
    def _():
        for rows in slabs:
            x = x_ref[rows, :]
            xn_ref[rows, :] = _rms(x, g_ref[...]).astype(BF16)
            o_ref[rows, :] = x

    for src_ref, dst_ref in zip(cast_in, cast_out):
        dst_ref[...] = src_ref[...].astype(BF16)

    for r in range(0, tm, half):
        rows = slice(r, r + half)
        xn = xn_ref[rows, :]
        gate = jnp.dot(xn, wg_ref[...], preferred_element_type=F32)
        up = jnp.dot(xn, wu_ref[...], preferred_element_type=F32)
        hid = (jax.nn.silu(gate) * up * 0.5).astype(BF16)
        o_ref[rows, :] += jnp.dot(hid, wd_ref[...], preferred_element_type=F32)

    @pl.when(f == pl.num_programs(1) - 1)
    def _():
        if emit_h:
            @pl.when(i > 0)
            def _():
                h_copy(i - 1).wait()

        if post_norm:
            for rows in slabs:
                yn = _rms(o_ref[rows, :], pg_ref[...])
                if emit_h:
                    h_buf[rows, :] = yn.astype(BF16)
                else:
                    o_ref[rows, :] = yn

        if emit_h:
            h_copy(i).start()

            @pl.when(i == pl.num_programs(0) - 1)
            def _():
                h_copy(i).wait()


def _ffn(x, g, wg, wu, wd, post_g=None, casts=(), *, keep_resid, tm=1024, tf=512, half=512):
    assert keep_resid or post_g is not None
    emit_h = keep_resid and post_g is not None
    n, d = x.shape
    dff = wg.shape[1]
    ni, nf = n // tm, dff // tf
    row = lambda i, f: (i, 0)
    vec = pl.BlockSpec((1, d), lambda i, f: (0, 0))
    in_specs = [pl.BlockSpec((tm, d), row), vec,
                pl.BlockSpec((d, tf), lambda i, f: (0, f)),
                pl.BlockSpec((d, tf), lambda i, f: (0, f)),
                pl.BlockSpec((tf, d), lambda i, f: (f, 0))]
    args = [x, g, wg, wu, wd]
    if post_g is not None:
        in_specs.append(vec)
        args.append(post_g)
    cast_specs = []
    for a in casts:
        rws, cls = a.shape
        if cls % nf == 0 and (cls // nf) % LANES == 0 and rws % ni == 0:
            cast_specs.append(pl.BlockSpec((rws // ni, cls // nf), lambda i, f: (i, f)))
        else:
            assert rws % (ni * nf) == 0
            cast_specs.append(pl.BlockSpec((rws // (ni * nf), cls),
                                           lambda i, f: (i * nf + f, 0)))
    out_specs = [pl.BlockSpec((tm, d), row)]
    out_shape = [jax.ShapeDtypeStruct((n, d), F32)]
    scratch = [pltpu.VMEM((tm, d), BF16)]
    if emit_h:
        out_specs.append(pl.BlockSpec(memory_space=pl.ANY))
        out_shape.append(jax.ShapeDtypeStruct((n, d), BF16))
        scratch += [pltpu.VMEM((tm, d), BF16), pltpu.SemaphoreType.DMA(())]
    return pl.pallas_call(
        functools.partial(_ffn_body, post_norm=post_g is not None, keep_resid=keep_resid,
                          n_casts=len(casts), half=half),
        grid=(ni, nf),
        in_specs=in_specs + cast_specs,
        out_specs=out_specs + cast_specs,
        out_shape=out_shape + [jax.ShapeDtypeStruct(a.shape, BF16) for a in casts],
        scratch_shapes=scratch,
        compiler_params=_params(("arbitrary", "arbitrary")),
        name="ffn",
    )(*args, *casts)


def _regroup_body(a_ref, b_ref, o_ref, cut_ref, *, cut_blocks, gap):
    c = pl.program_id(0)

    @pl.when(c < cut_blocks)
    def _():
        o_ref[...] = a_ref[...].T.astype(BF16)

    @pl.when(c == cut_blocks - 1)
    def _():
        pad = jnp.zeros((cut_ref.shape[1] - gap, b_ref.shape[1]), F32)
        cut_ref[...] = jnp.concatenate([b_ref[...], pad], axis=0).T

    @pl.when(c >= cut_blocks)
    def _():
        rows = jnp.concatenate([a_ref[gap:, :], b_ref[...]], axis=0)
        o_ref[...] = rows.T.astype(BF16)


def _regroup(wt, *, cut_lo, gap, tc=512):
    width, d = wt.shape
    out_w = width - gap
    assert cut_lo % tc == 0 and cut_lo >= tc and out_w % tc == 0 and tc % gap == 0
    return pl.pallas_call(
        functools.partial(_regroup_body, cut_blocks=cut_lo // tc, gap=gap),
        grid=(out_w // tc,),
        in_specs=[pl.BlockSpec((tc, d), lambda c: (c, 0)),
                  pl.BlockSpec((gap, d), lambda c: ((c + 1) * (tc // gap), 0))],
        out_specs=[pl.BlockSpec((d, tc), lambda c: (0, c)),
                   pl.BlockSpec((d, LANES), lambda c: (0, 0))],
        out_shape=[jax.ShapeDtypeStruct((d, out_w), BF16),
                   jax.ShapeDtypeStruct((d, LANES), F32)],
        compiler_params=_params(("arbitrary",)),
        name="regroup",
    )(wt, wt)


def _matmul_body(x_ref, w_ref, o_ref, *, precision):
    o_ref[...] = jnp.dot(x_ref[...], w_ref[...], preferred_element_type=F32,
                         precision=precision).astype(o_ref.dtype)


def _matmul(x, w, out_dtype, *, tm, tn, name, precision=None):
    n, d = x.shape
    nc = w.shape[1]
    return pl.pallas_call(
        functools.partial(_matmul_body, precision=precision),
        grid=(n // tm, nc // tn),
        in_specs=[pl.BlockSpec((tm, d), lambda i, j: (i, 0)),
                  pl.BlockSpec((d, tn), lambda i, j: (0, j))],
        out_specs=pl.BlockSpec((tm, tn), lambda i, j: (i, j)),
        out_shape=jax.ShapeDtypeStruct((n, nc), out_dtype),
        compiler_params=_params(("parallel", "parallel")),
        name=name,
    )(x, w)


def _norm_matmul_body(x_ref, g_ref, w_ref, o_ref):
    xn = _rms(x_ref[...], g_ref[...]).astype(BF16)
    o_ref[...] = jnp.dot(xn, w_ref[...], preferred_element_type=F32).astype(o_ref.dtype)


def _norm_matmul(x, g, w, out_dtype, *, tm, name):
    n, d = x.shape
    nc = w.shape[1]
    return pl.pallas_call(
        _norm_matmul_body,
        grid=(n // tm,),
        in_specs=[pl.BlockSpec((tm, d), lambda i: (i, 0)),
                  pl.BlockSpec((1, d), lambda i: (0, 0)),
                  pl.BlockSpec((d, nc), lambda i: (0, 0))],
        out_specs=pl.BlockSpec((tm, nc), lambda i: (i, 0)),
        out_shape=jax.ShapeDtypeStruct((n, nc), out_dtype),
        compiler_params=_params(("parallel",)),
        name=name,
    )(x, g, w)


def _la_body(x_ref, w_ref, b_ref, *refs, n_casts):
    cast_in, o_ref, cast_out = refs[:n_casts], refs[n_casts], refs[n_casts + 1:]
    z = jnp.dot(x_ref[...], w_ref[...], preferred_element_type=F32) + b_ref[...]
    o_ref[...] = jax.nn.log_sigmoid(z) * (1.0 / GLA_TAU)
    for src_ref, dst_ref in zip(cast_in, cast_out):
        dst_ref[...] = src_ref[...].astype(BF16)


def _la_proj(x, w, bias, casts=(), *, tm=1024):
    n, d = x.shape
    nc = w.shape[1]
    ni = n // tm
    assert all(a.shape[0] % ni == 0 for a in casts)
    cast_specs = [pl.BlockSpec((a.shape[0] // ni, a.shape[1]), lambda i: (i, 0))
                  for a in casts]
    return pl.pallas_call(
        functools.partial(_la_body, n_casts=len(casts)),
        grid=(ni,),
        in_specs=[pl.BlockSpec((tm, d), lambda i: (i, 0)),
                  pl.BlockSpec((d, nc), lambda i: (0, 0)),
                  pl.BlockSpec((1, nc), lambda i: (0, 0))] + cast_specs,
        out_specs=[pl.BlockSpec((tm, nc), lambda i: (i, 0))] + cast_specs,
        out_shape=[jax.ShapeDtypeStruct((n, nc), F32)]
                  + [jax.ShapeDtypeStruct(a.shape, BF16) for a in casts],
        compiler_params=_params(("parallel",)),
        name="la_proj",
    )(x, w, bias, *casts)


def _gla_body(q_ref, k_ref, v_ref, r_ref, la_ref, on_ref, *refs, heads, chunk, sub, n_casts):
    cast_in, y_ref, cast_out = refs[:n_casts], refs[n_casts], refs[n_casts + 1:2 * n_casts + 1]
    st_ref, kb_ref = refs[2 * n_casts + 1:]
    for src_ref, dst_ref in zip(cast_in, cast_out):
        dst_ref[...] = src_ref[...].astype(BF16)
    ts = q_ref.shape[0]
    dk = q_ref.shape[1] // heads
    dv = v_ref.shape[1] // heads
    scale = dk ** -0.5

    @pl.when(pl.program_id(1) == 0)
    def _():
        st_ref[...] = jnp.zeros_like(st_ref)

    t_idx = lax.broadcasted_iota(jnp.int32, (chunk, chunk), 0)
    s_idx = lax.broadcasted_iota(jnp.int32, (chunk, chunk), 1)
    causal = t_idx >= s_idx
    ones_tril = causal.astype(BF16)
    lane_in_sub = lax.broadcasted_iota(jnp.int32, (sub, chunk), 1) % sub

    halves = []
    m = chunk // 2
    while m >= sub:
        halves.append(m)
        m //= 2
    lane_half = {m: lax.broadcasted_iota(jnp.int32, (m, chunk), 1) >> (m.bit_length() - 1)
                 for m in halves}

    for c in range(ts // chunk):
        rows = slice(c * chunk, (c + 1) * chunk)
        kb = kb_ref.at[c % 2]
        la = la_ref[rows, :] * LOG2_E
        la_hi = la.astype(BF16)
        la_lo = (la - la_hi.astype(F32)).astype(BF16)
        b_all = (jnp.dot(ones_tril, la_hi, preferred_element_type=F32)
                 + jnp.dot(ones_tril, la_lo, preferred_element_type=F32))
        for h in range(heads):
            kcols = slice(h * dk, (h + 1) * dk)
            vcols = slice(h * dv, (h + 1) * dv)
            b = b_all[:, kcols]
            q = q_ref[rows, kcols].astype(F32) * scale
            k = k_ref[rows, kcols].astype(F32)
            v = v_ref[rows, vcols]
            st = st_ref[h]
            inter = lax.dot_general((q * jnp.exp2(b)).astype(BF16), st.astype(BF16),
                                    NT_DIMS, preferred_element_type=F32)

            kb[h, 0] = k
            kb[h, 1] = b
            diag = []
            for i in range(chunk // sub):
                lo = i * sub
                qi, bi = q[lo:lo + sub], b[lo:lo + sub]
                d = jnp.zeros((sub, chunk), F32)
                for s in range(sub):
                    ks = jnp.broadcast_to(kb[h, 0, lo + s:lo + s + 1, :], (sub, dk))
                    bs = jnp.broadcast_to(kb[h, 1, lo + s:lo + s + 1, :], (sub, dk))
                    col = jnp.sum(qi * ks * jnp.exp2(bi - bs), axis=-1, keepdims=True)
                    d = jnp.where(lane_in_sub == s, col, d)
                diag.append(d)
            scores = jnp.concatenate(diag, axis=0)

            for m in halves:
                parts = []
                for j in range(0, chunk, 2 * m):
                    b_mid = b[j + m - 1:j + m]
                    parts.append(k[j:j + m] * jnp.exp2(b_mid - b[j:j + m]))
                    parts.append(q[j + m:j + 2 * m] * jnp.exp2(b[j + m:j + 2 * m] - b_mid))
                z = jnp.concatenate(parts, axis=0).astype(BF16)
                pair = lax.dot_general(z, z, NT_DIMS, preferred_element_type=F32)
                out = []
                for j in range(0, chunk, 2 * m):
                    out.append(scores[j:j + m])
                    out.append(jnp.where(lane_half[m] == j // m, pair[j + m:j + 2 * m],
                                         scores[j + m:j + 2 * m]))
                scores = jnp.concatenate(out, axis=0)
            scores = jnp.where(causal, scores, 0.0)
            o = inter + jnp.dot(scores.astype(BF16), v, preferred_element_type=F32)

            b_end = b[chunk - 1:chunk]
            k_end = (k * jnp.exp2(b_end - b)).astype(BF16)
            upd = lax.dot_general(v, k_end, TN_DIMS, preferred_element_type=F32)
            st_ref[h] = st * jnp.exp2(b_end) + upd

            o = o * lax.rsqrt(jnp.mean(o * o, axis=-1, keepdims=True) + EPS)
            o = o * on_ref[:, vcols] * jax.nn.silu(r_ref[rows, vcols].astype(F32))
            y_ref[rows, vcols] = o.astype(y_ref.dtype)


def _gla(proj, la, onorm, casts=(), *, batch, seq, ts=1024):
    hk = la.shape[1]
    hv = onorm.shape[1]
    ns = seq // ts
    row = lambda b, s: b * ns + s
    steps = batch * ns
    assert all(a.shape[0] % steps == 0 for a in casts)
    cast_specs = [pl.BlockSpec((a.shape[0] // steps, a.shape[1]), lambda b, s: (row(b, s), 0))
                  for a in casts]
    return pl.pallas_call(
        functools.partial(_gla_body, heads=GLA_HEADS, chunk=GLA_CHUNK, sub=GLA_SUB,
                          n_casts=len(casts)),
        grid=(batch, ns),
        in_specs=[
            pl.BlockSpec((ts, hk), lambda b, s: (row(b, s), 0)),
            pl.BlockSpec((ts, hk), lambda b, s: (row(b, s), 1)),
            pl.BlockSpec((ts, hv), lambda b, s: (row(b, s), hk * 2 // hv)),
            pl.BlockSpec((ts, hv), lambda b, s: (row(b, s), hk * 2 // hv + 1)),
            pl.BlockSpec((ts, hk), lambda b, s: (row(b, s), 0)),
            pl.BlockSpec((1, hv), lambda b, s: (0, 0)),
        ] + cast_specs,
        out_specs=[pl.BlockSpec((ts, hv), lambda b, s: (row(b, s), 0))] + cast_specs,
        out_shape=[jax.ShapeDtypeStruct((batch * seq, hv), BF16)]
                  + [jax.ShapeDtypeStruct(a.shape, BF16) for a in casts],
        scratch_shapes=[pltpu.VMEM((GLA_HEADS, hv // GLA_HEADS, hk // GLA_HEADS), F32),
                        pltpu.VMEM((2, GLA_HEADS, 2, GLA_CHUNK, hk // GLA_HEADS), F32)],
        compiler_params=_params(("parallel", "arbitrary")),
        name="gla",
    )(proj, proj, proj, proj, la, onorm, *casts)


def _gelu(x):
    return 0.5 * x * (1.0 + lax.erf(x * (2.0 ** -0.5)))


def _sg_body(u_ref, v_ref, lg_ref, lb_ref, ws_ref, bs_ref, y_ref, *, groups, chunk):
    ts = u_ref.shape[0]
    gdim = u_ref.shape[1] // groups
    rr = lax.broadcasted_iota(jnp.int32, (chunk, chunk), 0)
    cc = lax.broadcasted_iota(jnp.int32, (chunk, chunk), 1)
    for g in range(groups):
        cols = slice(g * gdim, (g + 1) * gdim)
        w = jnp.where(rr >= cc, ws_ref[g], 0.0).astype(BF16)
        for c in range(ts // chunk):
            rows = slice(c * chunk, (c + 1) * chunk)
            u = _gelu(u_ref[rows, cols].astype(F32))
            v = _gelu(v_ref[rows, cols].astype(F32))
            mu = jnp.mean(v, axis=-1, keepdims=True)
            var = jnp.mean((v - mu) ** 2, axis=-1, keepdims=True)
            vn = (v - mu) * lax.rsqrt(var + EPS) * lg_ref[g:g + 1, :] + lb_ref[g:g + 1, :]
            vs = jnp.dot(w, vn.astype(BF16), preferred_element_type=F32) + bs_ref[g]
            y_ref[rows, cols] = (u * vs).astype(y_ref.dtype)


def _xa_body(q_ref, k_ref, v_ref, y_ref, *, heads):
    dh = q_ref.shape[1] // heads
    for h in range(heads):
        cols = slice(h * dh, (h + 1) * dh)
        s = lax.dot_general(q_ref[:, cols], k_ref[:, cols], NT_DIMS,
                            preferred_element_type=F32) * (dh ** -0.5)
        p = jnp.exp(s - jnp.max(s, axis=-1, keepdims=True))
        p = p / jnp.sum(p, axis=-1, keepdims=True)
        y_ref[:, cols] = jnp.dot(p.astype(BF16), v_ref[:, cols],
                                 preferred_element_type=F32).astype(y_ref.dtype)


def _merge_body(su0_ref, sv0_ref, xq0_ref, mk0_ref, mv0_ref,
                su_ref, sv_ref, xq_ref, mk_ref, mv_ref,
                ya_ref, ga_ref, gb_ref, gc_ref, lg_ref, lb_ref, ws_ref, bs_ref,
                wb_ref, wo_ref, x_ref, o_ref,
                yb_cur, yc_cur, yb_nxt, yc_nxt, *, groups, chunk, heads):
    def branches(u_ref, v_ref, q_ref, k_ref, w_ref, yb_ref, yc_ref):
        _sg_body(u_ref, v_ref, lg_ref, lb_ref, ws_ref, bs_ref, yb_ref, groups=groups, chunk=chunk)
        _xa_body(q_ref, k_ref, w_ref, yc_ref, heads=heads)

    @pl.when(pl.program_id(0) == 0)
    def _():
        branches(su0_ref, sv0_ref, xq0_ref, mk0_ref, mv0_ref, yb_cur, yc_cur)

    merged = None
    for i, (y_ref, g_ref) in enumerate(((ya_ref, ga_ref), (yb_cur, gb_ref), (yc_cur, gc_ref))):
        t = jnp.dot(y_ref[...], wb_ref[i], preferred_element_type=F32)
        t = jax.nn.sigmoid(g_ref[...].astype(F32)) * t
        merged = t if merged is None else merged + t
    o_ref[...] = x_ref[...] + jnp.dot(merged.astype(BF16), wo_ref[...],
                                      preferred_element_type=F32)

    branches(su_ref, sv_ref, xq_ref, mk_ref, mv_ref, yb_nxt, yc_nxt)
    yb_cur[...] = yb_nxt[...]
    yc_cur[...] = yc_nxt[...]


def _merge(ya, proj, kv, ln_g, ln_b, w_s, b_s, wb, wo, x, *, u_block, q_block, gate_block,
           seq, tm=256):
    n, d = x.shape
    bw = ya.shape[1]
    groups, gdim = ln_g.shape
    chunk = w_s.shape[1]
    mem_len = kv.shape[0] // (n // seq)
    tiles_per_seq = seq // tm
    last = n // tm - 1
    assert tm % chunk == 0 and groups * gdim == bw and kv.shape[1] == 2 * bw
    row = lambda i: (i, 0)
    nxt = lambda i: jnp.minimum(i + 1, last)
    const2 = lambda i: (0, 0)
    const3 = lambda i: (0, 0, 0)
    once = pl.Buffered(1)
    tile = lambda f: pl.BlockSpec((tm, bw), f)
    mem = lambda f: pl.BlockSpec((mem_len, bw), f)
    return pl.pallas_call(
        functools.partial(_merge_body, groups=groups, chunk=chunk, heads=XA_HEADS),
        grid=(n // tm,),
        in_specs=[
            tile(lambda i: (0, u_block)), tile(lambda i: (0, u_block + 1)),
            tile(lambda i: (0, q_block)),
            mem(lambda i: (0, 0)), mem(lambda i: (0, 1)),
            tile(lambda i: (nxt(i), u_block)), tile(lambda i: (nxt(i), u_block + 1)),
            tile(lambda i: (nxt(i), q_block)),
            mem(lambda i: (nxt(i) // tiles_per_seq, 0)), mem(lambda i: (nxt(i) // tiles_per_seq, 1)),
            tile(row),
            pl.BlockSpec((tm, d), lambda i: (i, gate_block)),
            pl.BlockSpec((tm, d), lambda i: (i, gate_block + 1)),
            pl.BlockSpec((tm, d), lambda i: (i, gate_block + 2)),
            pl.BlockSpec((groups, gdim), const2),
            pl.BlockSpec((groups, gdim), const2),
            pl.BlockSpec((groups, chunk, chunk), const3),
            pl.BlockSpec((groups, chunk, 1), const3),
            pl.BlockSpec(wb.shape, const3, pipeline_mode=once),
            pl.BlockSpec(wo.shape, const2, pipeline_mode=once),
            pl.BlockSpec((tm, d), row),
        ],
        out_specs=pl.BlockSpec((tm, d), row),
        out_shape=jax.ShapeDtypeStruct((n, d), F32),
        scratch_shapes=[pltpu.VMEM((tm, bw), BF16)] * 4,
        compiler_params=_params(("arbitrary",)),
        name="merge",
    )(proj, proj, proj, kv, kv, proj, proj, proj, kv, kv, ya, proj, proj, proj,
      ln_g, ln_b, w_s, b_s, wb, wo, x)


def kernel(x, mem, ffn1_norm, ffn1_w_gate, ffn1_w_up, ffn1_w_down, mix_norm, mem_norm, w_in, gla_w_gate_up, gla_gate_bias, gla_out_norm, sg_ln_g, sg_ln_b, sg_w_s, sg_b_s, w_kv_mem, w_branch, w_out, ffn2_norm, ffn2_w_gate, ffn2_w_up, ffn2_w_down, final_norm):
    batch, seq, d = x.shape
    depth = w_in.shape[0]
    hk = gla_w_gate_up.shape[2]
    hv = gla_out_norm.shape[1] * gla_out_norm.shape[2]
    bw = d // 2
    rank = gla_w_gate_up.shape[1]
    lr_lo = 2 * hk + 2 * hv

    xf = x.reshape(batch * seq, d)
    memf = mem.reshape(-1, d)
    for l in range(depth):
        last = l == depth - 1
        x1, h = _ffn(
            xf, ffn1_norm[l][None], ffn1_w_gate[l].astype(BF16), ffn1_w_up[l].astype(BF16),
            ffn1_w_down[l].astype(BF16), post_g=mix_norm[l][None], keep_resid=True)

        w_in_t = jnp.swapaxes(w_in[l], 0, 1)
        w_main, w_lr = _regroup(w_in_t, cut_lo=lr_lo, gap=rank)
        proj = _matmul(h, w_main, BF16, tm=1024, tn=2048, name="proj")
        w_up = jnp.pad(gla_w_gate_up[l], ((0, LANES - rank), (0, 0)))
        w_la = _matmul(w_lr, w_up, BF16, tm=d, tn=hk, name="fold_gate", precision=HIGHEST)
        la, wb, wo, wkv = _la_proj(h, w_la, gla_gate_bias[l][None],
                                   casts=(w_branch[l].reshape(-1, d), w_out[l], w_kv_mem[l]))
        kv = _norm_matmul(memf, mem_norm[l][None], wkv, BF16, tm=512, name="mem_kv")

        y_gla, wg2, wu2, wd2 = _gla(
            proj, la, gla_out_norm[l].reshape(1, hv),
            casts=(ffn2_w_gate[l], ffn2_w_up[l], ffn2_w_down[l]), batch=batch, seq=seq)
        x2 = _merge(y_gla, proj, kv, sg_ln_g[l], sg_ln_b[l], sg_w_s[l], sg_b_s[l][:, :, None],
                    wb.reshape(w_branch[l].shape), wo, x1,
                    u_block=lr_lo // bw, q_block=lr_lo // bw + 2,
                    gate_block=(lr_lo + 3 * bw) // d, seq=seq)

        (xf,) = _ffn(x2, ffn2_norm[l][None], wg2, wu2, wd2,
                     post_g=final_norm[None] if last else None, keep_resid=not last)
    return xf.reshape(batch, seq, d)
```
